```python
import math
import jax
import jax.numpy as jnp
from jax import lax
import numpy as np

D_MODEL = 1024
BATCH = 2
SEQ = 8192
DEPTH = 2

GRID_W = 64
CTX_LEN = 256
HEAD_DIM = 64
BRANCH_WIDTH = D_MODEL // 2
N_BRANCH = 4
DIFF_HEADS = BRANCH_WIDTH // (2 * HEAD_DIM)
NA_HEADS = BRANCH_WIDTH // HEAD_DIM
NA_KH = 8
NA_KW = 16
GQA_Q_HEADS = BRANCH_WIDTH // HEAD_DIM
GQA_KV_HEADS = 2
SWA_Q_HEADS = BRANCH_WIDTH // HEAD_DIM
SWA_KV_HEADS = 2
SWA_WINDOW = 128
Q_BLOCK = 128
D_FF = ((8 * D_MODEL // 3 + 255) // 256) * 256
FFN_RES_WEIGHT = 0.5
ROPE_THETA = 10000.0
NORM_EPS = 1e-6
N_MOD = 9

kernel_name = "hybrid_gated_branch_dit_block"


def _segments():
    widths = [
        ("a_q", DIFF_HEADS * 2 * HEAD_DIM), ("a_k", DIFF_HEADS * 2 * HEAD_DIM), ("a_v", DIFF_HEADS * 2 * HEAD_DIM),
        ("b_q", NA_HEADS * HEAD_DIM), ("b_k", NA_HEADS * HEAD_DIM), ("b_v", NA_HEADS * HEAD_DIM),
        ("c_q", GQA_Q_HEADS * HEAD_DIM), ("c_k", GQA_KV_HEADS * HEAD_DIM), ("c_v", GQA_KV_HEADS * HEAD_DIM),
        ("d_q", SWA_Q_HEADS * HEAD_DIM), ("d_k", SWA_KV_HEADS * HEAD_DIM), ("d_v", SWA_KV_HEADS * HEAD_DIM),
    ] + [("gate%d" % i, D_MODEL) for i in range(N_BRANCH)]
    seg, off = {}, 0
    for name, w in widths:
        seg[name] = (off, off + w)
        off += w
    return seg, off


def rms_norm(x, g):
    xf = x.astype(jnp.float32)
    y = xf * lax.rsqrt(jnp.mean(jnp.square(xf), axis=-1, keepdims=True) + NORM_EPS)
    return (y * g.astype(jnp.float32)).astype(x.dtype)


def swiglu(u, wg, wu, wd):
    return (jax.nn.silu(u @ wg) * (u @ wu)) @ wd


def rope_2d_tables(n_tok):
    t = jnp.arange(n_tok, dtype=jnp.int32)
    row = (t // GRID_W).astype(jnp.float32)
    col = (t % GRID_W).astype(jnp.float32)
    n_freq = HEAD_DIM // 4
    inv_freq = ROPE_THETA ** (-jnp.arange(n_freq, dtype=jnp.float32) / n_freq)
    ang = jnp.stack([row[:, None] * inv_freq, col[:, None] * inv_freq], axis=1)
    return jnp.cos(ang), jnp.sin(ang)


def apply_rope_2d(x, cos, sin):
    xf = x.astype(jnp.float32).reshape(x.shape[:-1] + (2, 2, HEAD_DIM // 4))
    x1, x2 = xf[..., 0, :], xf[..., 1, :]
    out = jnp.stack([x1 * cos - x2 * sin, x2 * cos + x1 * sin], axis=-2)
    return out.reshape(x.shape).astype(x.dtype)


def _heads(t, n):
    b, s, _ = t.shape
    return t.reshape(b, s, n, -1).transpose(0, 2, 1, 3)


def _gqa_heads(t, n_kv):
    b, s, _ = t.shape
    return t.reshape(b, s, n_kv, -1, HEAD_DIM).transpose(0, 2, 3, 1, 4)


def _merge(t):
    b, h, s, d = t.shape
    return t.transpose(0, 2, 1, 3).reshape(b, s, h * d)


def _merge_gqa(t):
    b, n, g, s, d = t.shape
    return t.transpose(0, 3, 1, 2, 4).reshape(b, s, n * g * d)


def _diff_split(t):
    b, s, _ = t.shape
    t = t.reshape(b, s, DIFF_HEADS, 2, HEAD_DIM).transpose(3, 0, 2, 1, 4)
    return t[0], t[1]


def _to_blocks(t, axis):
    shp = t.shape
    t = t.reshape(shp[:axis] + (shp[axis] // Q_BLOCK, Q_BLOCK) + shp[axis + 1:])
    return jnp.moveaxis(t, axis, 0)


def _from_blocks(t, axis):
    t = jnp.moveaxis(t, 0, axis)
    shp = t.shape
    return t.reshape(shp[:axis] + (shp[axis] * shp[axis + 1],) + shp[axis + 2:])


def _diff_attend(q1, q2, k1, k2, v, lam):
    scale = HEAD_DIM ** -0.5
    s1 = jnp.einsum('bhqd,bhkd->bhqk', q1, k1).astype(jnp.float32) * scale
    s2 = jnp.einsum('bhqd,bhkd->bhqk', q2, k2).astype(jnp.float32) * scale
    a = jax.nn.softmax(s1, axis=-1) - lam * jax.nn.softmax(s2, axis=-1)
    return jnp.einsum('bhqk,bhkd->bhqd', a.astype(v.dtype), v)


def _mha(q, k, v):
    s = jnp.einsum('bhqd,bhkd->bhqk', q, k).astype(jnp.float32) * HEAD_DIM ** -0.5
    p = jax.nn.softmax(s, axis=-1).astype(v.dtype)
    return jnp.einsum('bhqk,bhkd->bhqd', p, v)


def _gqa_attend(q, k, v, sink_ng=None):
    s = jnp.einsum('bngqd,bnkd->bngqk', q, k).astype(jnp.float32) * HEAD_DIM ** -0.5
    if sink_ng is None:
        p = jax.nn.softmax(s, axis=-1)
    else:
        sk = jnp.broadcast_to(sink_ng[None, :, :, None, None], s.shape[:-1] + (1,))
        p = jax.nn.softmax(jnp.concatenate([s, sk], axis=-1), axis=-1)[..., :-1]
    return jnp.einsum('bngqk,bnkd->bngqd', p.astype(v.dtype), v)


def neighbourhood_attention(q, k, v, k_ctx, v_ctx, rel_bias):
    b, h, s, d = q.shape
    rows = s // GRID_W
    kh, kw = min(NA_KH, rows), min(NA_KW, GRID_W)
    scale = d ** -0.5
    qg = q.reshape(b, h, rows, GRID_W, d)
    kg = k.reshape(b, h, rows, GRID_W, d)
    vg = v.reshape(b, h, rows, GRID_W, d)
    r_idx = jnp.arange(rows, dtype=jnp.int32)
    row_start = jnp.clip(r_idx - kh // 2, 0, rows - kh)
    cq = jnp.arange(GRID_W, dtype=jnp.int32)
    col_idx = jnp.clip(cq - kw // 2, 0, GRID_W - kw)[:, None] + jnp.arange(kw, dtype=jnp.int32)
    col_bias_idx = col_idx - cq[:, None] + (NA_KW - 1)

    def one_row(args):
        q_row, r, r0 = args
        k_slab = lax.dynamic_slice_in_dim(kg, r0, kh, axis=2)
        v_slab = lax.dynamic_slice_in_dim(vg, r0, kh, axis=2)
        k_win = k_slab[:, :, :, col_idx]
        v_win = v_slab[:, :, :, col_idx]
        s_loc = jnp.einsum('bhqd,bhrqkd->bhqrk', q_row, k_win).astype(jnp.float32) * scale
        row_bias_idx = r0 + jnp.arange(kh, dtype=jnp.int32) - r + (NA_KH - 1)
        bias = rel_bias[:, row_bias_idx[None, :, None], col_bias_idx[:, None, :]]
        s_loc = (s_loc + bias.astype(jnp.float32)[None]).reshape(b, h, GRID_W, kh * kw)
        s_ctx = jnp.einsum('bhqd,bhcd->bhqc', q_row, k_ctx).astype(jnp.float32) * scale
        p = jax.nn.softmax(jnp.concatenate([s_loc, s_ctx], axis=-1), axis=-1).astype(v.dtype)
        p_loc = p[..., :kh * kw].reshape(b, h, GRID_W, kh, kw)
        return (jnp.einsum('bhqrk,bhrqkd->bhqd', p_loc, v_win)
                + jnp.einsum('bhqc,bhcd->bhqd', p[..., kh * kw:], v_ctx))

    out = lax.map(one_row, (jnp.moveaxis(qg, 2, 0), r_idx, row_start))
    return jnp.moveaxis(out, 0, 2).reshape(b, h, s, d)


def sliding_window_attention(q, k, v, k_ctx, v_ctx, sink_ng):
    b, n, g, s, d = q.shape
    nb = s // Q_BLOCK
    scale = d ** -0.5
    qb = q.reshape(b, n, g, nb, Q_BLOCK, d)

    def band(t):
        tp = jnp.pad(t.reshape(b, n, nb, Q_BLOCK, d), ((0, 0), (0, 0), (1, 1), (0, 0), (0, 0)))
        return jnp.concatenate([tp[:, :, :-2], tp[:, :, 1:-1], tp[:, :, 2:]], axis=3)

    kb, vb = band(k), band(v)
    qi = jnp.arange(Q_BLOCK, dtype=jnp.int32)
    kj = jnp.arange(3 * Q_BLOCK, dtype=jnp.int32)
    in_window = jnp.abs(kj[None, :] - Q_BLOCK - qi[:, None]) <= SWA_WINDOW
    kpos = (jnp.arange(nb, dtype=jnp.int32)[:, None] - 1) * Q_BLOCK + kj[None, :]
    mask = in_window[None] & ((kpos >= 0) & (kpos < s))[:, None, :]
    s_band = jnp.einsum('bngxqd,bnxkd->bngxqk', qb, kb).astype(jnp.float32) * scale
    s_band = jnp.where(mask, s_band, -jnp.inf)
    s_ctx = jnp.einsum('bngxqd,bncd->bngxqc', qb, k_ctx).astype(jnp.float32) * scale
    s_sink = jnp.broadcast_to(sink_ng[None, :, :, None, None, None], s_ctx.shape[:-1] + (1,))
    p = jax.nn.softmax(jnp.concatenate([s_band, s_ctx, s_sink], axis=-1), axis=-1).astype(v.dtype)
    nk, nc = 3 * Q_BLOCK, k_ctx.shape[2]
    o = (jnp.einsum('bngxqk,bnxkd->bngxqd', p[..., :nk], vb)
         + jnp.einsum('bngxqc,bncd->bngxqd', p[..., nk:nk + nc], v_ctx))
    return o.reshape(b, n, g, s, d)


def token_mixer(u_x, u_c, w_in, w_branch, w_out, diff_lambda, diff_subln, na_bias, qk_gain, sink,
                lam_init, cos, sin, need_ctx):
    seg, _ = _segments()

    def proj(u, name):
        a, bnd = seg[name]
        return u @ w_in[:, a:bnd]

    ys_x, ys_c = [], []

    dl = diff_lambda.astype(jnp.float32)
    lam = jnp.exp(jnp.sum(dl[0] * dl[1])) - jnp.exp(jnp.sum(dl[2] * dl[3])) + lam_init
    q1, q2 = _diff_split(proj(u_x, 'a_q'))
    k1, k2 = _diff_split(proj(u_x, 'a_k'))
    k1c, k2c = _diff_split(proj(u_c, 'a_k'))
    vac = _heads(proj(u_c, 'a_v'), DIFF_HEADS)
    q1, q2 = apply_rope_2d(q1, cos, sin), apply_rope_2d(q2, cos, sin)
    k1_all = jnp.concatenate([apply_rope_2d(k1, cos, sin), k1c], axis=2)
    k2_all = jnp.concatenate([apply_rope_2d(k2, cos, sin), k2c], axis=2)
    va_all = jnp.concatenate([_heads(proj(u_x, 'a_v'), DIFF_HEADS), vac], axis=2)
    o = lax.map(lambda qq: _diff_attend(qq[0], qq[1], k1_all, k2_all, va_all, lam),
                (_to_blocks(q1, 2), _to_blocks(q2, 2)))
    ys_x.append(_merge(rms_norm(_from_blocks(o, 2), diff_subln) * (1.0 - lam_init)))
    if need_ctx:
        q1c, q2c = _diff_split(proj(u_c, 'a_q'))
        oc = _diff_attend(q1c, q2c, k1c, k2c, vac, lam)
        ys_c.append(_merge(rms_norm(oc, diff_subln) * (1.0 - lam_init)))

    kbc = _heads(proj(u_c, 'b_k'), NA_HEADS)
    vbc = _heads(proj(u_c, 'b_v'), NA_HEADS)
    ob = neighbourhood_attention(_heads(proj(u_x, 'b_q'), NA_HEADS), _heads(proj(u_x, 'b_k'), NA_HEADS),
                                 _heads(proj(u_x, 'b_v'), NA_HEADS), kbc, vbc, na_bias)
    ys_x.append(_merge(ob))
    if need_ctx:
        ys_c.append(_merge(_mha(_heads(proj(u_c, 'b_q'), NA_HEADS), kbc, vbc)))

    qc_ = apply_rope_2d(rms_norm(_gqa_heads(proj(u_x, 'c_q'), GQA_KV_HEADS), qk_gain[0]), cos, sin)
    kc_ = apply_rope_2d(rms_norm(_heads(proj(u_x, 'c_k'), GQA_KV_HEADS), qk_gain[1]), cos, sin)
    kcc = rms_norm(_heads(proj(u_c, 'c_k'), GQA_KV_HEADS), qk_gain[1])
    vcc = _heads(proj(u_c, 'c_v'), GQA_KV_HEADS)
    kc_all = jnp.concatenate([kc_, kcc], axis=2)
    vc_all = jnp.concatenate([_heads(proj(u_x, 'c_v'), GQA_KV_HEADS), vcc], axis=2)
    oc_ = lax.map(lambda qb: _gqa_attend(qb, kc_all, vc_all), _to_blocks(qc_, 3))
    ys_x.append(_merge_gqa(_from_blocks(oc_, 3)))
    if need_ctx:
        qcc = rms_norm(_gqa_heads(proj(u_c, 'c_q'), GQA_KV_HEADS), qk_gain[0])
        ys_c.append(_merge_gqa(_gqa_attend(qcc, kcc, vcc)))

    sink_ng = sink.astype(jnp.float32).reshape(SWA_KV_HEADS, SWA_Q_HEADS // SWA_KV_HEADS)
    kdc = _heads(proj(u_c, 'd_k'), SWA_KV_HEADS)
    vdc = _heads(proj(u_c, 'd_v'), SWA_KV_HEADS)
    od = sliding_window_attention(
        apply_rope_2d(_gqa_heads(proj(u_x, 'd_q'), SWA_KV_HEADS), cos, sin),
        apply_rope_2d(_heads(proj(u_x, 'd_k'), SWA_KV_HEADS), cos, sin),
        _heads(proj(u_x, 'd_v'), SWA_KV_HEADS), kdc, vdc, sink_ng)
    ys_x.append(_merge_gqa(od))
    if need_ctx:
        ys_c.append(_merge_gqa(_gqa_attend(_gqa_heads(proj(u_c, 'd_q'), SWA_KV_HEADS), kdc, vdc, sink_ng)))

    def merge_branches(u, ys):
        acc = jax.nn.sigmoid(proj(u, 'gate0')) * (ys[0] @ w_branch[0])
        for i in range(1, N_BRANCH):
            acc = acc + jax.nn.sigmoid(proj(u, 'gate%d' % i)) * (ys[i] @ w_branch[i])
        return acc @ w_out

    y_x = merge_branches(u_x, ys_x)
    y_c = merge_branches(u_c, ys_c) if need_ctx else None
    return y_x, y_c


def _ffn_sublayer(h, mod, k, g_pre, g_post, wg, wu, wd):
    u = rms_norm(h, g_pre) * (1.0 + mod[:, :, 3 * k + 1]) + mod[:, :, 3 * k]
    return h + FFN_RES_WEIGHT * mod[:, :, 3 * k + 2] * rms_norm(swiglu(u, wg, wu, wd), g_post)


def setup_inputs(seed: int = 0) -> dict:
    key = jax.random.key(seed)
    ks = jax.random.split(key, 20)
    _, in_width = _segments()
    f32 = jnp.float32

    def w(k, shape, fan_in, mult=1.0):
        return jax.random.normal(k, shape, f32) * (mult * fan_in ** -0.5)

    def gain(k, shape):
        return 1.0 + 0.05 * jax.random.normal(k, shape, f32)

    return {
        "x": jax.random.normal(ks[0], (BATCH, SEQ, D_MODEL), f32),
        "c": jax.random.normal(ks[1], (BATCH, D_MODEL), f32),
        "ctx": jax.random.normal(ks[2], (BATCH, CTX_LEN, D_MODEL), f32),
        "c_ctx": jax.random.normal(ks[3], (D_MODEL,), f32),
        "w_mod": w(ks[4], (DEPTH, D_MODEL, N_MOD * D_MODEL), D_MODEL, 0.5),
        "b_mod": 0.02 * jax.random.normal(ks[5], (DEPTH, N_MOD * D_MODEL), f32),
        "norm_gain": gain(ks[6], (DEPTH, 6, D_MODEL)),
        "ffn_w_gate": w(ks[7], (DEPTH, 2, D_MODEL, D_FF), D_MODEL),
        "ffn_w_up": w(ks[8], (DEPTH, 2, D_MODEL, D_FF), D_MODEL),
        "ffn_w_down": w(ks[9], (DEPTH, 2, D_FF, D_MODEL), D_FF),
        "w_in": w(ks[10], (DEPTH, D_MODEL, in_width), D_MODEL),
        "w_branch": w(ks[11], (DEPTH, N_BRANCH, BRANCH_WIDTH, D_MODEL), BRANCH_WIDTH),
        "w_out": w(ks[12], (DEPTH, D_MODEL, D_MODEL), D_MODEL),
        "diff_lambda": 0.1 * jax.random.normal(ks[13], (DEPTH, 4, HEAD_DIM), f32),
        "diff_subln": gain(ks[14], (DEPTH, 2 * HEAD_DIM)),
        "na_bias": 0.1 * jax.random.normal(ks[15], (DEPTH, NA_HEADS, 2 * NA_KH - 1, 2 * NA_KW - 1), f32),
        "qk_norm": gain(ks[16], (DEPTH, 2, HEAD_DIM)),
        "sink": 0.5 * jax.random.normal(ks[17], (DEPTH, SWA_Q_HEADS), f32),
    }


def reference(x, c, ctx, c_ctx, w_mod, b_mod, norm_gain, ffn_w_gate, ffn_w_up, ffn_w_down, w_in, w_branch,
              w_out, diff_lambda, diff_subln, na_bias, qk_norm, sink):
    bsz, n_tok, d = x.shape
    cos, sin = rope_2d_tables(n_tok)
    h_x, h_c = x, ctx
    sc, scc = jax.nn.silu(c), jax.nn.silu(c_ctx)
    for l in range(DEPTH):
        last = l == DEPTH - 1
        lam_init = 0.8 - 0.6 * math.exp(-0.3 * l)
        mod_x = (sc @ w_mod[l] + b_mod[l]).reshape(bsz, 1, N_MOD, d)
        mod_c = (scc @ w_mod[l] + b_mod[l]).reshape(1, 1, N_MOD, d)
        g = norm_gain[l]
        h_x = _ffn_sublayer(h_x, mod_x, 0, g[0], g[1], ffn_w_gate[l, 0], ffn_w_up[l, 0], ffn_w_down[l, 0])
        h_c = _ffn_sublayer(h_c, mod_c, 0, g[0], g[1], ffn_w_gate[l, 0], ffn_w_up[l, 0], ffn_w_down[l, 0])
        u_x = rms_norm(h_x, g[2]) * (1.0 + mod_x[:, :, 4]) + mod_x[:, :, 3]
        u_c = rms_norm(h_c, g[2]) * (1.0 + mod_c[:, :, 4]) + mod_c[:, :, 3]
        y_x, y_c = token_mixer(u_x, u_c, w_in[l], w_branch[l], w_out[l], diff_lambda[l], diff_subln[l],
                               na_bias[l], qk_norm[l], sink[l], lam_init, cos, sin, not last)
        h_x = h_x + mod_x[:, :, 5] * rms_norm(y_x, g[3])
        h_x = _ffn_sublayer(h_x, mod_x, 2, g[4], g[5], ffn_w_gate[l, 1], ffn_w_up[l, 1], ffn_w_down[l, 1])
        if not last:
            h_c = h_c + mod_c[:, :, 5] * rms_norm(y_c, g[3])
            h_c = _ffn_sublayer(h_c, mod_c, 2, g[4], g[5], ffn_w_gate[l, 1], ffn_w_up[l, 1], ffn_w_down[l, 1])
    return h_x
```

```python
import functools
import math

import jax
import jax.numpy as jnp
from jax import lax
from jax.experimental import pallas as pl
from jax.experimental.pallas import tpu as pltpu

HEAD_DIM = 64
LANES = 128
GRID_W = 64
NA_KH = 8
NA_KW = 16
NA_QROWS = 4
NA_SLAB = NA_QROWS + NA_KH
SWA_WINDOW = 128
FFN_RES_WEIGHT = 0.5
ROPE_THETA = 10000.0
NORM_EPS = 1e-6
N_MOD = 9
NEG = -1e30
VMEM_LIMIT = 48 * 1024 * 1024
BF16 = jnp.bfloat16
F32 = jnp.float32

P_AQ, P_AK, P_AV = 0, 4, 8
P_BQ, P_BK, P_BV = 12, 16, 20
P_CQ, P_CK, P_CV = 24, 28, 30
P_DQ, P_DK, P_DV = 32, 36, 38
N_PAIRS = 40
SEGMENTS = (
    (P_AQ, 4, None, True, True), (P_AK, 4, None, True, False), (P_AV, 4, None, False, False),
    (P_BQ, 4, None, False, True), (P_BK, 4, None, False, False), (P_BV, 4, None, False, False),
    (P_CQ, 4, 0, True, True), (P_CK, 2, 1, True, False), (P_CV, 2, None, False, False),
    (P_DQ, 4, None, True, True), (P_DK, 2, None, True, False), (P_DV, 2, None, False, False),
)
Q_SCALE = HEAD_DIM ** -0.5


def _const_spec(shape):
    zeros = (0,) * len(shape)
    return pl.BlockSpec(shape, lambda *_: zeros, pipeline_mode=pl.Buffered(1))


def _params(*sem):
    return pltpu.CompilerParams(dimension_semantics=sem, vmem_limit_bytes=VMEM_LIMIT)


def _rms(x):
    return x * lax.rsqrt(jnp.mean(x * x, axis=-1, keepdims=True) + NORM_EPS)


def _sigmoid(x):
    return 1.0 / (1.0 + jnp.exp(-x))


def _dot(a, b):
    return jnp.dot(a, b, preferred_element_type=F32)


def _dot_nt(a, b):
    return lax.dot_general(a, b, (((1,), (1,)), ((), ())), preferred_element_type=F32)


def _mod_kernel(ct_ref, w_ref, b_ref, o_ref, *, n_rows):
    ct = ct_ref[...]
    s = ct * _sigmoid(ct)
    w = w_ref[...]
    for r in range(n_rows):
        o_ref[r:r + 1, :] = jnp.sum(w * s[:, r:r + 1], axis=0, keepdims=True) + b_ref[...]


def _modulation(cond_t, w_mod, b_mod, n_rows):
    d, n = w_mod.shape
    tn = 1024
    return pl.pallas_call(
        functools.partial(_mod_kernel, n_rows=n_rows),
        grid=(n // tn,),
        in_specs=[pl.BlockSpec((d, 8), lambda j: (0, 0)),
                  pl.BlockSpec((d, tn), lambda j: (0, j)),
                  pl.BlockSpec((1, tn), lambda j: (0, j))],
        out_specs=pl.BlockSpec((n_rows, tn), lambda j: (0, j)),
        out_shape=jax.ShapeDtypeStruct((n_rows, n), F32),
        compiler_params=_params("parallel"),
    )(cond_t, w_mod, b_mod.reshape(1, n))


def _ffn_kernel(h_ref, mod_ref, g_ref, wg_ref, wu_ref, wd_ref, o_ref, *, k, n_chunks):
    h = h_ref[...]
    mod = mod_ref[...]
    g = g_ref[...]
    u = (_rms(h) * g[2 * k:2 * k + 1] * (1.0 + mod[3 * k + 1:3 * k + 2]) + mod[3 * k:3 * k + 1]).astype(BF16)
    cf = wg_ref.shape[1] // n_chunks
    y = None
    for c in range(n_chunks):
        gate = _dot(u, wg_ref[:, c * cf:(c + 1) * cf])
        up = _dot(u, wu_ref[:, c * cf:(c + 1) * cf])
        act = (gate * _sigmoid(gate) * up).astype(BF16)
        part = _dot(act, wd_ref[c * cf:(c + 1) * cf, :])
        y = part if y is None else y + part
    o_ref[...] = h + FFN_RES_WEIGHT * mod[3 * k + 2:3 * k + 3] * (_rms(y) * g[2 * k + 1:2 * k + 2])


def _ffn(h, mod, gains, wg, wu, wd, *, k, mod_row, tm, n_tiles=None):
    bsz, t, d = h.shape
    n_tiles = t // tm if n_tiles is None else n_tiles
    f = wg.shape[1]
    return pl.pallas_call(
        functools.partial(_ffn_kernel, k=k, n_chunks=2),
        grid=(bsz, n_tiles),
        in_specs=[pl.BlockSpec((None, tm, d), lambda b, i: (b, i, 0)),
                  pl.BlockSpec((None, N_MOD, d), lambda b, i: (mod_row(b), 0, 0)),
                  _const_spec((6, d)), _const_spec((d, f)), _const_spec((d, f)), _const_spec((f, d))],
        out_specs=pl.BlockSpec((None, tm, d), lambda b, i: (b, i, 0)),
        out_shape=jax.ShapeDtypeStruct((bsz, n_tiles * tm, d), F32),
        compiler_params=_params("parallel", "parallel"),
    )(h, mod, gains, wg, wu, wd)


def _inproj_kernel(*refs, rope):
    if rope:
        h_ref, mod_ref, g_ref, w_ref, qkg_ref, gm_ref, cos_ref, sa_ref, sb_ref, o_ref = refs
    else:
        h_ref, mod_ref, g_ref, w_ref, qkg_ref, gm_ref, o_ref = refs
    h = h_ref[...]
    mod = mod_ref[...]
    u = (_rms(h) * g_ref[2:3, :] * (1.0 + mod[4:5]) + mod[3:4]).astype(BF16)
    for p0, n_p, norm_row, rotary, scaled in SEGMENTS:
        r = _dot(u, w_ref[:, p0 * LANES:(p0 + n_p) * LANES])
        for j in range(n_p):
            x = r[:, j * LANES:(j + 1) * LANES]
            if norm_row is not None:
                xx = x * x
                hi = xx.astype(BF16)
                lo = (xx - hi.astype(F32)).astype(BF16)
                ms = _dot(hi, gm_ref[...]) + _dot(lo, gm_ref[...])
                x = x * lax.rsqrt(ms + NORM_EPS) * qkg_ref[norm_row:norm_row + 1, :]
            if rotary and rope:
                x = (x * cos_ref[...] + pltpu.roll(x, LANES - 16, 1) * sa_ref[...]
                     + pltpu.roll(x, 16, 1) * sb_ref[...])
            if scaled:
                x = x * Q_SCALE
            o_ref[p0 + j] = x.astype(BF16)


def _inproj(h, mod, gains, w_qkv, qk_gain, gmat, rope_tabs, *, mod_row, tm):
    bsz, t, d = h.shape
    rope = rope_tabs is not None
    in_specs = [pl.BlockSpec((None, tm, d), lambda b, i: (b, i, 0)),
                pl.BlockSpec((None, N_MOD, d), lambda b, i: (mod_row(b), 0, 0)),
                _const_spec((6, d)), _const_spec(w_qkv.shape), _const_spec((2, LANES)),
                _const_spec((LANES, LANES))]
    args = [h, mod, gains, w_qkv, qk_gain, gmat]
    if rope:
        in_specs += [pl.BlockSpec((tm, LANES), lambda b, i: (i, 0))] * 3
        args += list(rope_tabs)
    return pl.pallas_call(
        functools.partial(_inproj_kernel, rope=rope),
        grid=(bsz, t // tm),
        in_specs=in_specs,
        out_specs=pl.BlockSpec((None, N_PAIRS, tm, LANES), lambda b, i: (b, 0, i, 0)),
        out_shape=jax.ShapeDtypeStruct((bsz, N_PAIRS, t, LANES), BF16),
        compiler_params=_params("parallel", "parallel"),
    )(*args)


def _half_masks(shape):
    lane = lax.broadcasted_iota(jnp.int32, shape, len(shape) - 1)
    return lane < HEAD_DIM


def _stack_heads(q_ref, n_qp):
    parts = []
    for p in range(n_qp):
        q = q_ref[p]
        lo = _half_masks(q.shape)
        parts += [jnp.where(lo, q, jnp.zeros_like(q)), jnp.where(lo, jnp.zeros_like(q), q)]
    return jnp.concatenate(parts, axis=0)


def _store_heads(o_ref, o, n_qp, tq):
    for p in range(n_qp):
        even = o[(2 * p) * tq:(2 * p + 1) * tq]
        odd = o[(2 * p + 1) * tq:(2 * p + 2) * tq]
        o_ref[p] = jnp.where(_half_masks(even.shape), even, odd).astype(o_ref.dtype)


def _sink_column(sink_ref, kv_head, n_heads, tq):
    row = lax.broadcasted_iota(jnp.int32, (n_heads * tq, 1), 0)
    col = jnp.full((n_heads * tq, 1), sink_ref[kv_head * n_heads], F32)
    for gq in range(1, n_heads):
        col = jnp.where(row >= gq * tq, sink_ref[kv_head * n_heads + gq], col)
    return col


def _full_attn_kernel(*refs, n_qp, use_x, use_sink, diff, tk, lam_init):
    refs = list(refs)
    q_ref = refs.pop(0)
    kx_ref, vx_ref = (refs.pop(0), refs.pop(0)) if use_x else (None, None)
    kc_ref, vc_ref = refs.pop(0), refs.pop(0)
    sink_ref = refs.pop(0) if use_sink else None
    dl_ref, sub_ref = (refs.pop(0), refs.pop(0)) if diff else (None, None)
    o_ref, m_ref, l_ref, acc_ref = refs
    tq = q_ref.shape[1]
    n_heads = 2 * n_qp
    qs = _stack_heads(q_ref, n_qp)

    if use_sink:
        m_ref[...] = _sink_column(sink_ref, pl.program_id(1), n_heads, tq)
        l_ref[...] = jnp.ones(l_ref.shape, F32)
    else:
        m_ref[...] = jnp.full(m_ref.shape, NEG, F32)
        l_ref[...] = jnp.zeros(l_ref.shape, F32)
    acc_ref[...] = jnp.zeros(acc_ref.shape, F32)

    def step(k, v):
        s = _dot_nt(qs, k)
        m_prev = m_ref[...]
        m_new = jnp.maximum(m_prev, jnp.max(s, axis=-1, keepdims=True))
        alpha = jnp.exp(m_prev - m_new)
        p = jnp.exp(s - m_new)
        l_ref[...] = alpha * l_ref[...] + jnp.sum(p, axis=-1, keepdims=True)
        acc_ref[...] = alpha * acc_ref[...] + _dot(p.astype(BF16), v)
        m_ref[...] = m_new

    if use_x:
        def body(j, carry):
            off = pl.multiple_of(j * tk, tk)
            step(kx_ref[pl.ds(off, tk), :], vx_ref[pl.ds(off, tk), :])
            return carry
        lax.fori_loop(0, kx_ref.shape[0] // tk, body, 0)
    step(kc_ref[...], vc_ref[...])

    o = acc_ref[...] / l_ref[...]
    if diff:
        dl = dl_ref[...]
        lam = (jnp.exp(jnp.sum(dl[0:1] * dl[1:2], axis=-1, keepdims=True))
               - jnp.exp(jnp.sum(dl[2:3] * dl[3:4], axis=-1, keepdims=True)) + lam_init)
        d = o[:tq] - lam * o[tq:]
        o_ref[0] = (_rms(d) * sub_ref[...] * (1.0 - lam_init)).astype(o_ref.dtype)
    else:
        _store_heads(o_ref, o, n_qp, tq)


def _full_attn(q_arr, q_pair0, kv_x, kv_c, k_pair0, v_pair0, *, n_qp, tq, n_groups, sink=None,
               diff_lambda=None, diff_subln=None, lam_init=0.0, tk=1024):
    bsz, _, t_q, _ = q_arr.shape
    c_len = kv_c.shape[2]
    use_x = kv_x is not None
    diff = diff_lambda is not None
    m_rows = 2 * n_qp * tq
    qp_blk = q_pair0 // n_qp
    in_specs = [pl.BlockSpec((None, n_qp, tq, LANES), lambda b, g, i: (b, qp_blk + g, i, 0))]
    args = [q_arr]
    if use_x:
        s_len = kv_x.shape[2]
        tk = min(tk, s_len)
        in_specs += [pl.BlockSpec((None, None, s_len, LANES), lambda b, g, i: (b, k_pair0 + g, 0, 0)),
                     pl.BlockSpec((None, None, s_len, LANES), lambda b, g, i: (b, v_pair0 + g, 0, 0))]
        args += [kv_x, kv_x]
    in_specs += [pl.BlockSpec((None, None, c_len, LANES), lambda b, g, i: (b, k_pair0 + g, 0, 0)),
                 pl.BlockSpec((None, None, c_len, LANES), lambda b, g, i: (b, v_pair0 + g, 0, 0))]
    args += [kv_c, kv_c]
    if sink is not None:
        in_specs.append(pl.BlockSpec(memory_space=pltpu.SMEM))
        args.append(sink)
    if diff:
        in_specs += [_const_spec((4, HEAD_DIM)), _const_spec((1, LANES))]
        args += [diff_lambda, diff_subln.reshape(1, LANES)]
    return pl.pallas_call(
        functools.partial(_full_attn_kernel, n_qp=n_qp, use_x=use_x, use_sink=sink is not None, diff=diff,
                          tk=tk, lam_init=lam_init),
        grid=(bsz, n_groups, t_q // tq),
        in_specs=in_specs,
        out_specs=pl.BlockSpec((None, n_qp, tq, LANES), lambda b, g, i: (b, g, i, 0)),
        out_shape=jax.ShapeDtypeStruct((bsz, n_groups * n_qp, t_q, LANES), BF16),
        scratch_shapes=[pltpu.VMEM((m_rows, 1), F32), pltpu.VMEM((m_rows, 1), F32),
                        pltpu.VMEM((m_rows, LANES), F32)],
        compiler_params=_params("parallel", "parallel", "parallel"),
    )(*args)


def _na_kernel(q_ref, kx_ref, vx_ref, kc_ref, vc_ref, bias_ref, o_ref, *, grid_rows):
    j = pl.program_id(2)
    tq = q_ref.shape[0]
    n_keys = NA_SLAB * GRID_W
    row0 = jnp.clip(NA_QROWS * j - NA_KH // 2, 0, grid_rows - NA_SLAB)
    start = pl.multiple_of(row0 * GRID_W, GRID_W)
    k_slab = kx_ref[pl.ds(start, n_keys), :]
    v_slab = vx_ref[pl.ds(start, n_keys), :]
    q = q_ref[...]
    lo = _half_masks(q.shape)
    qs = jnp.concatenate([jnp.where(lo, q, jnp.zeros_like(q)), jnp.where(lo, jnp.zeros_like(q), q)], axis=0)
    s_loc = _dot_nt(qs, k_slab) + bias_ref[...].reshape(2 * tq, n_keys)
    s_ctx = _dot_nt(qs, kc_ref[...])
    m = jnp.maximum(jnp.max(s_loc, axis=-1, keepdims=True), jnp.max(s_ctx, axis=-1, keepdims=True))
    p_loc = jnp.exp(s_loc - m)
    p_ctx = jnp.exp(s_ctx - m)
    l = jnp.sum(p_loc, axis=-1, keepdims=True) + jnp.sum(p_ctx, axis=-1, keepdims=True)
    o = (_dot(p_loc.astype(BF16), v_slab) + _dot(p_ctx.astype(BF16), vc_ref[...])) / l
    o_ref[...] = jnp.where(lo, o[:tq], o[tq:]).astype(o_ref.dtype)


def _na_bias_tables(na_bias, grid_rows):
    n_blk = grid_rows // NA_QROWS
    cq = jnp.arange(GRID_W, dtype=jnp.int32)
    kc = jnp.arange(GRID_W, dtype=jnp.int32)
    c0 = jnp.clip(cq - NA_KW // 2, 0, GRID_W - NA_KW)
    col_ok = (kc[None, :] >= c0[:, None]) & (kc[None, :] < c0[:, None] + NA_KW)
    dc = jnp.clip(kc[None, :] - cq[:, None] + (NA_KW - 1), 0, 2 * NA_KW - 2)
    tabs = []
    for j in (0, 1, n_blk - 1):
        row0 = min(max(NA_QROWS * j - NA_KH // 2, 0), grid_rows - NA_SLAB)
        r = NA_QROWS * j + jnp.arange(NA_QROWS, dtype=jnp.int32)
        r0 = jnp.clip(r - NA_KH // 2, 0, grid_rows - NA_KH)
        kr = row0 + jnp.arange(NA_SLAB, dtype=jnp.int32)
        row_ok = (kr[None, :] >= r0[:, None]) & (kr[None, :] < r0[:, None] + NA_KH)
        dr = jnp.clip(kr[None, :] - r[:, None] + (NA_KH - 1), 0, 2 * NA_KH - 2)
        vals = na_bias[:, dr[:, None, :, None], dc[None, :, None, :]]
        ok = row_ok[:, None, :, None] & col_ok[None, :, None, :]
        tab = jnp.where(ok[None], vals, NEG)
        tabs.append(tab.reshape(na_bias.shape[0], NA_QROWS * GRID_W, NA_SLAB * GRID_W))
    return jnp.stack(tabs).astype(F32)


def _na_attn(qkv_x, qkv_c, bias_tabs):
    bsz, _, s_len, _ = qkv_x.shape
    c_len = qkv_c.shape[2]
    grid_rows = s_len // GRID_W
    n_blk = grid_rows // NA_QROWS
    tq = NA_QROWS * GRID_W
    n_keys = NA_SLAB * GRID_W

    def cfg(j):
        return jnp.where(j == 0, 0, jnp.where(j == n_blk - 1, 2, 1))

    return pl.pallas_call(
        functools.partial(_na_kernel, grid_rows=grid_rows),
        grid=(bsz, 4, n_blk),
        in_specs=[pl.BlockSpec((None, None, tq, LANES), lambda b, p, j: (b, P_BQ + p, j, 0)),
                  pl.BlockSpec((None, None, s_len, LANES), lambda b, p, j: (b, P_BK + p, 0, 0)),
                  pl.BlockSpec((None, None, s_len, LANES), lambda b, p, j: (b, P_BV + p, 0, 0)),
                  pl.BlockSpec((None, None, c_len, LANES), lambda b, p, j: (b, P_BK + p, 0, 0)),
                  pl.BlockSpec((None, None, c_len, LANES), lambda b, p, j: (b, P_BV + p, 0, 0)),
                  pl.BlockSpec((None, 2, tq, n_keys), lambda b, p, j: (cfg(j), p, 0, 0))],
        out_specs=pl.BlockSpec((None, None, tq, LANES), lambda b, p, j: (b, p, j, 0)),
        out_shape=jax.ShapeDtypeStruct((bsz, 4, s_len, LANES), BF16),
        compiler_params=_params("parallel", "parallel", "parallel"),
    )(qkv_x, qkv_x, qkv_x, qkv_c, qkv_c, bias_tabs)


def _swa_kernel(q_ref, kx_ref, vx_ref, kc_ref, vc_ref, sink_ref, o_ref):
    n_qp, tq, _ = q_ref.shape
    n_heads = 2 * n_qp
    s_len = kx_ref.shape[0]
    n_keys = tq + 2 * SWA_WINDOW
    q0 = pl.program_id(2) * tq
    start = pl.multiple_of(jnp.clip(q0 - SWA_WINDOW, 0, s_len - n_keys), SWA_WINDOW)
    k_slab = kx_ref[pl.ds(start, n_keys), :]
    v_slab = vx_ref[pl.ds(start, n_keys), :]
    qs = _stack_heads(q_ref, n_qp)
    qpos = q0 + lax.broadcasted_iota(jnp.int32, (tq, n_keys), 0)
    kpos = start + lax.broadcasted_iota(jnp.int32, (tq, n_keys), 1)
    band = jnp.where(jnp.abs(kpos - qpos) <= SWA_WINDOW, 0.0, NEG).astype(F32)
    s_loc = (_dot_nt(qs, k_slab).reshape(n_heads, tq, n_keys) + band[None]).reshape(n_heads * tq, n_keys)
    s_ctx = _dot_nt(qs, kc_ref[...])
    sink = _sink_column(sink_ref, pl.program_id(1), n_heads, tq)
    m = jnp.maximum(jnp.maximum(jnp.max(s_loc, axis=-1, keepdims=True), jnp.max(s_ctx, axis=-1, keepdims=True)),
                    sink)
    p_loc = jnp.exp(s_loc - m)
    p_ctx = jnp.exp(s_ctx - m)
    l = jnp.sum(p_loc, axis=-1, keepdims=True) + jnp.sum(p_ctx, axis=-1, keepdims=True) + jnp.exp(sink - m)
    o = (_dot(p_loc.astype(BF16), v_slab) + _dot(p_ctx.astype(BF16), vc_ref[...])) / l
    _store_heads(o_ref, o, n_qp, tq)


def _swa_attn(qkv_x, qkv_c, sink, *, tq=256):
    bsz, _, s_len, _ = qkv_x.shape
    c_len = qkv_c.shape[2]
    n_qp = 2
    return pl.pallas_call(
        _swa_kernel,
        grid=(bsz, 2, s_len // tq),
        in_specs=[pl.BlockSpec((None, n_qp, tq, LANES), lambda b, g, i: (b, P_DQ // n_qp + g, i, 0)),
                  pl.BlockSpec((None, None, s_len, LANES), lambda b, g, i: (b, P_DK + g, 0, 0)),
                  pl.BlockSpec((None, None, s_len, LANES), lambda b, g, i: (b, P_DV + g, 0, 0)),
                  pl.BlockSpec((None, None, c_len, LANES), lambda b, g, i: (b, P_DK + g, 0, 0)),
                  pl.BlockSpec((None, None, c_len, LANES), lambda b, g, i: (b, P_DV + g, 0, 0)),
                  pl.BlockSpec(memory_space=pltpu.SMEM)],
        out_specs=pl.BlockSpec((None, n_qp, tq, LANES), lambda b, g, i: (b, g, i, 0)),
        out_shape=jax.ShapeDtypeStruct((bsz, 2 * n_qp, s_len, LANES), BF16),
        compiler_params=_params("parallel", "parallel", "parallel"),
    )(qkv_x, qkv_x, qkv_x, qkv_c, qkv_c, sink)


def _merge_kernel(h_ref, mod_ref, g_ref, ya_ref, yb_ref, yc_ref, yd_ref, wgate_ref, wbr_ref, wout_ref, o_ref):
    h = h_ref[...]
    mod = mod_ref[...]
    g = g_ref[...]
    d = h.shape[1]
    u = (_rms(h) * g[2:3] * (1.0 + mod[4:5]) + mod[3:4]).astype(BF16)
    acc = None
    for i, y_ref in enumerate((ya_ref, yb_ref, yc_ref, yd_ref)):
        y = jnp.concatenate([y_ref[p] for p in range(y_ref.shape[0])], axis=-1)
        term = _sigmoid(_dot(u, wgate_ref[:, i * d:(i + 1) * d])) * _dot(y, wbr_ref[i])
        acc = term if acc is None else acc + term
    out = _dot(acc.astype(BF16), wout_ref[...])
    o_ref[...] = h + mod[5:6] * (_rms(out) * g[3:4])


def _merge(h, mod, gains, ys, w_gate, w_branch, w_out, *, mod_row, tm):
    bsz, t, d = h.shape
    n_yp = ys[0].shape[1]
    y_spec = pl.BlockSpec((None, n_yp, tm, LANES), lambda b, i: (b, 0, i, 0))
    return pl.pallas_call(
        _merge_kernel,
        grid=(bsz, t // tm),
        in_specs=[pl.BlockSpec((None, tm, d), lambda b, i: (b, i, 0)),
                  pl.BlockSpec((None, N_MOD, d), lambda b, i: (mod_row(b), 0, 0)),
                  _const_spec((6, d)), y_spec, y_spec, y_spec, y_spec,
                  _const_spec(w_gate.shape), _const_spec(w_branch.shape), _const_spec(w_out.shape)],
        out_specs=pl.BlockSpec((None, tm, d), lambda b, i: (b, i, 0)),
        out_shape=jax.ShapeDtypeStruct((bsz, t, d), F32),
        compiler_params=_params("parallel", "parallel"),
    )(h, mod, gains, *ys, w_gate, w_branch, w_out)


def _rope_tables(n_tok):
    t = jnp.arange(n_tok, dtype=jnp.int32)
    row = (t // GRID_W).astype(F32)
    col = (t % GRID_W).astype(F32)
    n_freq = HEAD_DIM // 4
    inv_freq = ROPE_THETA ** (-jnp.arange(n_freq, dtype=F32) / n_freq)
    lane = jnp.arange(LANES, dtype=jnp.int32)
    axis = (lane % HEAD_DIM) // (HEAD_DIM // 2)
    freq = inv_freq[lane % n_freq]
    ang = jnp.where(axis[None, :] == 0, row[:, None], col[:, None]) * freq[None, :]
    first = (lane % (HEAD_DIM // 2)) < n_freq
    cos, sin = jnp.cos(ang), jnp.sin(ang)
    return cos, jnp.where(first[None, :], -sin, 0.0), jnp.where(first[None, :], 0.0, sin)


def _dup_heads(w):
    h0, h1 = w[:, :HEAD_DIM], w[:, HEAD_DIM:]
    return jnp.concatenate([h0, h0, h1, h1], axis=1)


def _qkv_weight(w_in):
    o = 0
    cols = []
    for name, width in (("a", 1536), ("b", 1536)):
        cols.append(w_in[:, o:o + width])
        o += width
    for _ in ("c", "d"):
        cols += [w_in[:, o:o + 512], _dup_heads(w_in[:, o + 512:o + 640]), _dup_heads(w_in[:, o + 640:o + 768])]
        o += 768
    return jnp.concatenate(cols, axis=1).astype(BF16), o


def kernel(x, c, ctx, c_ctx, w_mod, b_mod, norm_gain, ffn_w_gate, ffn_w_up, ffn_w_down, w_in, w_branch,
           w_out, diff_lambda, diff_subln, na_bias, qk_norm, sink):
    bsz, s_len, d = x.shape
    c_len = ctx.shape[1]
    depth = w_mod.shape[0]
    tm = min(512, s_len)
    cond_t = jnp.zeros((d, 8), F32).at[:, :bsz].set(c.T).at[:, bsz].set(c_ctx)
    rope_tabs = _rope_tables(s_len)
    grp = jnp.arange(LANES, dtype=jnp.int32) // HEAD_DIM
    gmat = jnp.where(grp[:, None] == grp[None, :], 1.0 / HEAD_DIM, 0.0).astype(BF16)
    x_row = lambda b: b
    c_row = lambda b: bsz

    h_x, h_c = x, ctx
    for l in range(depth):
        last = l == depth - 1
        lam_init = 0.8 - 0.6 * math.exp(-0.3 * l)
        mod = _modulation(cond_t, w_mod[l], b_mod[l], bsz + 1).reshape(bsz + 1, N_MOD, d)
        g = norm_gain[l]
        wg, wu, wd = ffn_w_gate[l].astype(BF16), ffn_w_up[l].astype(BF16), ffn_w_down[l].astype(BF16)
        w_qkv, gate_off = _qkv_weight(w_in[l])
        w_gate = w_in[l][:, gate_off:].astype(BF16)
        w_br = w_branch[l].astype(BF16)
        w_o = w_out[l].astype(BF16)
        qk_gain = jnp.tile(qk_norm[l], (1, 2))

        h_x = _ffn(h_x, mod, g, wg[0], wu[0], wd[0], k=0, mod_row=x_row, tm=tm)
        h_c = _ffn(h_c, mod, g, wg[0], wu[0], wd[0], k=0, mod_row=c_row, tm=c_len)

        qkv_x = _inproj(h_x, mod, g, w_qkv, qk_gain, gmat, rope_tabs, mod_row=x_row, tm=tm)
        qkv_c = _inproj(h_c, mod, g, w_qkv, qk_gain, gmat, None, mod_row=c_row, tm=c_len)

        ya = _full_attn(qkv_x, P_AQ, qkv_x, qkv_c, P_AK, P_AV, n_qp=1, tq=min(512, s_len), n_groups=4,
                        diff_lambda=diff_lambda[l], diff_subln=diff_subln[l], lam_init=lam_init)
        yb = _na_attn(qkv_x, qkv_c, _na_bias_tables(na_bias[l], s_len // GRID_W))
        yc = _full_attn(qkv_x, P_CQ, qkv_x, qkv_c, P_CK, P_CV, n_qp=2, tq=min(256, s_len), n_groups=2)
        yd = _swa_attn(qkv_x, qkv_c, sink[l])
        h_x = _merge(h_x, mod, g, (ya, yb, yc, yd), w_gate, w_br, w_o, mod_row=x_row, tm=tm)
        h_x = _ffn(h_x, mod, g, wg[1], wu[1], wd[1], k=2, mod_row=x_row, tm=tm)

        if not last:
            ya = _full_attn(qkv_c, P_AQ, None, qkv_c, P_AK, P_AV, n_qp=1, tq=c_len, n_groups=4,
                            diff_lambda=diff_lambda[l], diff_subln=diff_subln[l], lam_init=lam_init)
            yb = _full_attn(qkv_c, P_BQ, None, qkv_c, P_BK, P_BV, n_qp=1, tq=c_len, n_groups=4)
            yc = _full_attn(qkv_c, P_CQ, None, qkv_c, P_CK, P_CV, n_qp=2, tq=c_len, n_groups=2)
            yd = _full_attn(qkv_c, P_DQ, None, qkv_c, P_DK, P_DV, n_qp=2, tq=c_len, n_groups=2, sink=sink[l])
            h_c = _merge(h_c, mod, g, (ya, yb, yc, yd), w_gate, w_br, w_o, mod_row=c_row, tm=c_len)
            h_c = _ffn(h_c, mod, g, wg[1], wu[1], wd[1], k=2, mod_row=c_row, tm=c_len)
    return h_x
```

```python
import functools
import math

import jax
import jax.numpy as jnp
from jax import lax
from jax.experimental import pallas as pl
from jax.experimental.pallas import tpu as pltpu

HEAD_DIM = 64
LANES = 128
GRID_W = 64
NA_KH = 8
NA_KW = 16
NA_QROWS = 4
NA_SLAB = NA_QROWS + NA_KH
SWA_WINDOW = 128
ATTN_SUB_ROWS = 256
FFN_RES_WEIGHT = 0.5
ROPE_THETA = 10000.0
NORM_EPS = 1e-6
N_MOD = 9
NEG = -1e30
VMEM_LIMIT = 48 * 1024 * 1024
BF16 = jnp.bfloat16
F32 = jnp.float32

P_AQ, P_AK, P_AV = 0, 4, 8
P_BQ, P_BK, P_BV = 12, 16, 20
P_CQ, P_CK, P_CV = 24, 28, 30
P_DQ, P_DK, P_DV = 32, 36, 38
N_PAIRS = 40
SEGMENTS = (
    (P_AQ, 4, None, True, True), (P_AK, 4, None, True, False), (P_AV, 4, None, False, False),
    (P_BQ, 4, None, False, True), (P_BK, 4, None, False, False), (P_BV, 4, None, False, False),
    (P_CQ, 4, 0, True, True), (P_CK, 2, 1, True, False), (P_CV, 2, None, False, False),
    (P_DQ, 4, None, True, True), (P_DK, 2, None, True, False), (P_DV, 2, None, False, False),
)
Q_SCALE = HEAD_DIM ** -0.5


def _const_spec(shape):
    zeros = (0,) * len(shape)
    return pl.BlockSpec(shape, lambda *_: zeros, pipeline_mode=pl.Buffered(1))


def _params(*sem):
    return pltpu.CompilerParams(dimension_semantics=sem, vmem_limit_bytes=VMEM_LIMIT)


def _rms(x):
    return x * lax.rsqrt(jnp.mean(x * x, axis=-1, keepdims=True) + NORM_EPS)


def _sigmoid(x):
    return 1.0 / (1.0 + jnp.exp(-x))


def _dot(a, b):
    return jnp.dot(a, b, preferred_element_type=F32)


def _dot_nt(a, b):
    return lax.dot_general(a, b, (((1,), (1,)), ((), ())), preferred_element_type=F32)


def _mod_kernel(ct_ref, w_ref, b_ref, o_ref, *, n_rows):
    ct = ct_ref[...]
    s = ct * _sigmoid(ct)
    w = w_ref[...]
    for r in range(n_rows):
        o_ref[r:r + 1, :] = jnp.sum(w * s[:, r:r + 1], axis=0, keepdims=True) + b_ref[...]


def _modulation(cond_t, w_mod, b_mod, n_rows):
    d, n = w_mod.shape
    tn = 1024
    return pl.pallas_call(
        functools.partial(_mod_kernel, n_rows=n_rows),
        grid=(n // tn,),
        in_specs=[pl.BlockSpec((d, 8), lambda j: (0, 0)),
                  pl.BlockSpec((d, tn), lambda j: (0, j)),
                  pl.BlockSpec((1, tn), lambda j: (0, j))],
        out_specs=pl.BlockSpec((n_rows, tn), lambda j: (0, j)),
        out_shape=jax.ShapeDtypeStruct((n_rows, n), F32),
        compiler_params=_params("parallel"),
    )(cond_t, w_mod, b_mod.reshape(1, n))


def _ffn_kernel(h_ref, mod_ref, g_ref, wg_ref, wu_ref, wd_ref, o_ref, *, k, n_chunks):
    h = h_ref[...]
    mod = mod_ref[...]
    g = g_ref[...]
    u = (_rms(h) * g[2 * k:2 * k + 1] * (1.0 + mod[3 * k + 1:3 * k + 2]) + mod[3 * k:3 * k + 1]).astype(BF16)
    cf = wg_ref.shape[1] // n_chunks
    y = None
    for c in range(n_chunks):
        gate = _dot(u, wg_ref[:, c * cf:(c + 1) * cf])
        up = _dot(u, wu_ref[:, c * cf:(c + 1) * cf])
        act = (gate * _sigmoid(gate) * up).astype(BF16)
        part = _dot(act, wd_ref[c * cf:(c + 1) * cf, :])
        y = part if y is None else y + part
    o_ref[...] = h + FFN_RES_WEIGHT * mod[3 * k + 2:3 * k + 3] * (_rms(y) * g[2 * k + 1:2 * k + 2])


def _ffn(h, mod, gains, wg, wu, wd, *, k, mod_row, tm, n_tiles=None):
    bsz, t, d = h.shape
    n_tiles = t // tm if n_tiles is None else n_tiles
    f = wg.shape[1]
    return pl.pallas_call(
        functools.partial(_ffn_kernel, k=k, n_chunks=2),
        grid=(bsz, n_tiles),
        in_specs=[pl.BlockSpec((None, tm, d), lambda b, i: (b, i, 0)),
                  pl.BlockSpec((None, N_MOD, d), lambda b, i: (mod_row(b), 0, 0)),
                  _const_spec((6, d)), _const_spec((d, f)), _const_spec((d, f)), _const_spec((f, d))],
        out_specs=pl.BlockSpec((None, tm, d), lambda b, i: (b, i, 0)),
        out_shape=jax.ShapeDtypeStruct((bsz, n_tiles * tm, d), F32),
        compiler_params=_params("parallel", "parallel"),
    )(h, mod, gains, wg, wu, wd)


def _inproj_kernel(*refs, rope):
    if rope:
        h_ref, mod_ref, g_ref, w_ref, qkg_ref, gm_ref, cos_ref, sa_ref, sb_ref, o_ref = refs
    else:
        h_ref, mod_ref, g_ref, w_ref, qkg_ref, gm_ref, o_ref = refs
    h = h_ref[...]
    mod = mod_ref[...]
    u = (_rms(h) * g_ref[2:3, :] * (1.0 + mod[4:5]) + mod[3:4]).astype(BF16)
    for p0, n_p, norm_row, rotary, scaled in SEGMENTS:
        r = _dot(u, w_ref[:, p0 * LANES:(p0 + n_p) * LANES])
        for j in range(n_p):
            x = r[:, j * LANES:(j + 1) * LANES]
            if norm_row is not None:
                xx = x * x
                hi = xx.astype(BF16)
                lo = (xx - hi.astype(F32)).astype(BF16)
                ms = _dot(hi, gm_ref[...]) + _dot(lo, gm_ref[...])
                x = x * lax.rsqrt(ms + NORM_EPS) * qkg_ref[norm_row:norm_row + 1, :]
            if rotary and rope:
                x = (x * cos_ref[...] + pltpu.roll(x, LANES - 16, 1) * sa_ref[...]
                     + pltpu.roll(x, 16, 1) * sb_ref[...])
            if scaled:
                x = x * Q_SCALE
            o_ref[p0 + j] = x.astype(BF16)


def _inproj(h, mod, gains, w_qkv, qk_gain, gmat, rope_tabs, *, mod_row, tm):
    bsz, t, d = h.shape
    rope = rope_tabs is not None
    in_specs = [pl.BlockSpec((None, tm, d), lambda b, i: (b, i, 0)),
                pl.BlockSpec((None, N_MOD, d), lambda b, i: (mod_row(b), 0, 0)),
                _const_spec((6, d)), _const_spec(w_qkv.shape), _const_spec((2, LANES)),
                _const_spec((LANES, LANES))]
    args = [h, mod, gains, w_qkv, qk_gain, gmat]
    if rope:
        in_specs += [pl.BlockSpec((tm, LANES), lambda b, i: (i, 0))] * 3
        args += list(rope_tabs)
    return pl.pallas_call(
        functools.partial(_inproj_kernel, rope=rope),
        grid=(bsz, t // tm),
        in_specs=in_specs,
        out_specs=pl.BlockSpec((None, N_PAIRS, tm, LANES), lambda b, i: (b, 0, i, 0)),
        out_shape=jax.ShapeDtypeStruct((bsz, N_PAIRS, t, LANES), BF16),
        compiler_params=_params("parallel", "parallel"),
    )(*args)


def _half_masks(shape):
    lane = lax.broadcasted_iota(jnp.int32, shape, len(shape) - 1)
    return lane < HEAD_DIM


def _stack_heads(q_ref, n_qp):
    parts = []
    for p in range(n_qp):
        q = q_ref[p]
        lo = _half_masks(q.shape)
        parts += [jnp.where(lo, q, jnp.zeros_like(q)), jnp.where(lo, jnp.zeros_like(q), q)]
    return jnp.concatenate(parts, axis=0)


def _store_heads(o_ref, o, n_qp, tq):
    for p in range(n_qp):
        even = o[(2 * p) * tq:(2 * p + 1) * tq]
        odd = o[(2 * p + 1) * tq:(2 * p + 2) * tq]
        o_ref[p] = jnp.where(_half_masks(even.shape), even, odd).astype(o_ref.dtype)


def _sink_column(sink_ref, kv_head, n_heads, tq):
    row = lax.broadcasted_iota(jnp.int32, (n_heads * tq, 1), 0)
    col = jnp.full((n_heads * tq, 1), sink_ref[kv_head * n_heads], F32)
    for gq in range(1, n_heads):
        col = jnp.where(row >= gq * tq, sink_ref[kv_head * n_heads + gq], col)
    return col


def _full_attn_kernel(*refs, n_qp, use_x, use_sink, diff, tk, sub_rows, lam_init):
    refs = list(refs)
    q_ref = refs.pop(0)
    kx_ref, vx_ref = (refs.pop(0), refs.pop(0)) if use_x else (None, None)
    kc_ref, vc_ref = refs.pop(0), refs.pop(0)
    sink_ref = refs.pop(0) if use_sink else None
    dl_ref, sub_ref = (refs.pop(0), refs.pop(0)) if diff else (None, None)
    o_ref, m_ref, l_ref, acc_ref = refs
    tq = q_ref.shape[1]
    n_heads = 2 * n_qp
    qs = _stack_heads(q_ref, n_qp)

    if use_sink:
        m_ref[...] = _sink_column(sink_ref, pl.program_id(1), n_heads, tq)
        l_ref[...] = jnp.ones(l_ref.shape, F32)
    else:
        m_ref[...] = jnp.full(m_ref.shape, NEG, F32)
        l_ref[...] = jnp.zeros(l_ref.shape, F32)
    acc_ref[...] = jnp.zeros(acc_ref.shape, F32)

    def step(k, v):
        for r0 in range(0, qs.shape[0], sub_rows):
            rows = slice(r0, r0 + sub_rows)
            s = _dot_nt(qs[rows], k)
            m_prev = m_ref[rows]
            m_new = jnp.maximum(m_prev, jnp.max(s, axis=-1, keepdims=True))
            alpha = jnp.exp(m_prev - m_new)
            p = jnp.exp(s - m_new)
            l_ref[rows] = alpha * l_ref[rows] + jnp.sum(p, axis=-1, keepdims=True)
            acc_ref[rows] = alpha * acc_ref[rows] + _dot(p.astype(BF16), v)
            m_ref[rows] = m_new

    if use_x:
        def body(j, carry):
            off = pl.multiple_of(j * tk, tk)
            step(kx_ref[pl.ds(off, tk), :], vx_ref[pl.ds(off, tk), :])
            return carry
        lax.fori_loop(0, kx_ref.shape[0] // tk, body, 0)
    step(kc_ref[...], vc_ref[...])

    o = acc_ref[...] / l_ref[...]
    if diff:
        dl = dl_ref[...]
        lam = (jnp.exp(jnp.sum(dl[0:1] * dl[1:2], axis=-1, keepdims=True))
               - jnp.exp(jnp.sum(dl[2:3] * dl[3:4], axis=-1, keepdims=True)) + lam_init)
        d = o[:tq] - lam * o[tq:]
        o_ref[0] = (_rms(d) * sub_ref[...] * (1.0 - lam_init)).astype(o_ref.dtype)
    else:
        _store_heads(o_ref, o, n_qp, tq)


def _full_attn(q_arr, q_pair0, kv_x, kv_c, k_pair0, v_pair0, *, n_qp, tq, n_groups, sink=None,
               diff_lambda=None, diff_subln=None, lam_init=0.0, tk=1024):
    bsz, _, t_q, _ = q_arr.shape
    c_len = kv_c.shape[2]
    use_x = kv_x is not None
    diff = diff_lambda is not None
    m_rows = 2 * n_qp * tq
    qp_blk = q_pair0 // n_qp
    in_specs = [pl.BlockSpec((None, n_qp, tq, LANES), lambda b, g, i: (b, qp_blk + g, i, 0))]
    args = [q_arr]
    if use_x:
        s_len = kv_x.shape[2]
        tk = min(tk, s_len)
        in_specs += [pl.BlockSpec((None, None, s_len, LANES), lambda b, g, i: (b, k_pair0 + g, 0, 0)),
                     pl.BlockSpec((None, None, s_len, LANES), lambda b, g, i: (b, v_pair0 + g, 0, 0))]
        args += [kv_x, kv_x]
    in_specs += [pl.BlockSpec((None, None, c_len, LANES), lambda b, g, i: (b, k_pair0 + g, 0, 0)),
                 pl.BlockSpec((None, None, c_len, LANES), lambda b, g, i: (b, v_pair0 + g, 0, 0))]
    args += [kv_c, kv_c]
    if sink is not None:
        in_specs.append(pl.BlockSpec(memory_space=pltpu.SMEM))
        args.append(sink)
    if diff:
        in_specs += [_const_spec((4, HEAD_DIM)), _const_spec((1, LANES))]
        args += [diff_lambda, diff_subln.reshape(1, LANES)]
    return pl.pallas_call(
        functools.partial(_full_attn_kernel, n_qp=n_qp, use_x=use_x, use_sink=sink is not None, diff=diff,
                          tk=tk, sub_rows=min(ATTN_SUB_ROWS, m_rows), lam_init=lam_init),
        grid=(bsz, n_groups, t_q // tq),
        in_specs=in_specs,
        out_specs=pl.BlockSpec((None, n_qp, tq, LANES), lambda b, g, i: (b, g, i, 0)),
        out_shape=jax.ShapeDtypeStruct((bsz, n_groups * n_qp, t_q, LANES), BF16),
        scratch_shapes=[pltpu.VMEM((m_rows, 1), F32), pltpu.VMEM((m_rows, 1), F32),
                        pltpu.VMEM((m_rows, LANES), F32)],
        compiler_params=_params("parallel", "parallel", "parallel"),
    )(*args)


def _na_kernel(q_ref, kx_ref, vx_ref, kc_ref, vc_ref, bias_ref, o_ref, *, grid_rows):
    j = pl.program_id(2)
    tq = q_ref.shape[0]
    n_keys = NA_SLAB * GRID_W
    row0 = jnp.clip(NA_QROWS * j - NA_KH // 2, 0, grid_rows - NA_SLAB)
    start = pl.multiple_of(row0 * GRID_W, GRID_W)
    k_slab = kx_ref[pl.ds(start, n_keys), :]
    v_slab = vx_ref[pl.ds(start, n_keys), :]
    q = q_ref[...]
    lo = _half_masks(q.shape)
    qs = jnp.concatenate([jnp.where(lo, q, jnp.zeros_like(q)), jnp.where(lo, jnp.zeros_like(q), q)], axis=0)
    s_loc = _dot_nt(qs, k_slab) + bias_ref[...].reshape(2 * tq, n_keys)
    s_ctx = _dot_nt(qs, kc_ref[...])
    m = jnp.maximum(jnp.max(s_loc, axis=-1, keepdims=True), jnp.max(s_ctx, axis=-1, keepdims=True))
    p_loc = jnp.exp(s_loc - m)
    p_ctx = jnp.exp(s_ctx - m)
    l = jnp.sum(p_loc, axis=-1, keepdims=True) + jnp.sum(p_ctx, axis=-1, keepdims=True)
    o = (_dot(p_loc.astype(BF16), v_slab) + _dot(p_ctx.astype(BF16), vc_ref[...])) / l
    o_ref[...] = jnp.where(lo, o[:tq], o[tq:]).astype(o_ref.dtype)


def _na_bias_tables(na_bias, grid_rows):
    n_heads, n_dr, n_dc = na_bias.shape
    n_blk = grid_rows // NA_QROWS
    plan = []
    for j in (0, 1, n_blk - 1):
        row0 = min(max(NA_QROWS * j - NA_KH // 2, 0), grid_rows - NA_SLAB)
        rows = []
        for i in range(NA_QROWS):
            r = NA_QROWS * j + i
            r0 = min(max(r - NA_KH // 2, 0), grid_rows - NA_KH)
            rows.append(tuple((row0 + t - r + NA_KH - 1) if r0 <= row0 + t < r0 + NA_KH else None
                              for t in range(NA_SLAB)))
        plan.append(tuple(rows))
    padded = jnp.pad(na_bias, ((0, 0), (0, 16 - n_dr), (0, LANES - n_dc)))
    return pl.pallas_call(
        functools.partial(_na_bias_kernel, plan=tuple(plan)),
        grid=(n_heads,),
        in_specs=[pl.BlockSpec((None, 16, LANES), lambda h: (h, 0, 0))],
        out_specs=pl.BlockSpec((3, None, NA_QROWS * GRID_W, NA_SLAB * GRID_W), lambda h: (0, h, 0, 0)),
        out_shape=jax.ShapeDtypeStruct((3, n_heads, NA_QROWS * GRID_W, NA_SLAB * GRID_W), F32),
        compiler_params=_params("parallel"),
    )(padded)


def _na_bias_kernel(b_ref, o_ref, *, plan):
    shape = (GRID_W, LANES)
    lane = lax.broadcasted_iota(jnp.int32, shape, 1)
    cq = lax.broadcasted_iota(jnp.int32, shape, 0)
    kc = lane % GRID_W
    c0 = jnp.clip(cq - NA_KW // 2, 0, GRID_W - NA_KW)
    col_bias = jnp.where(kc >= c0, jnp.where(kc < c0 + NA_KW, 0.0, NEG), NEG).astype(F32)
    left_half = lane < GRID_W
    neg = jnp.full(shape, NEG, F32)
    used = sorted({dr for cfg in plan for row in cfg for dr in row if dr is not None})
    shift = LANES - (NA_KW - 1)
    tile_lo, tile_hi = {}, {}
    for dr in used:
        row = jnp.broadcast_to(b_ref[dr:dr + 1, :], shape)
        tile_lo[dr] = pltpu.roll(row, shift, 1, stride=1, stride_axis=0)
        tile_hi[dr] = pltpu.roll(row, (shift + GRID_W) % LANES, 1, stride=1, stride_axis=0)
    for cfg, cfg_rows in enumerate(plan):
        for i, row in enumerate(cfg_rows):
            for u in range(NA_SLAB // 2):
                left = neg if row[2 * u] is None else tile_lo[row[2 * u]]
                right = neg if row[2 * u + 1] is None else tile_hi[row[2 * u + 1]]
                blk = jnp.where(col_bias < 0.0, neg, jnp.where(left_half, left, right))
                o_ref[cfg, i * GRID_W:(i + 1) * GRID_W, u * LANES:(u + 1) * LANES] = blk


def _na_attn(qkv_x, qkv_c, bias_tabs):
    bsz, _, s_len, _ = qkv_x.shape
    c_len = qkv_c.shape[2]
    grid_rows = s_len // GRID_W
    n_blk = grid_rows // NA_QROWS
    tq = NA_QROWS * GRID_W
    n_keys = NA_SLAB * GRID_W

    def cfg(j):
        return jnp.where(j == 0, 0, jnp.where(j == n_blk - 1, 2, 1))

    return pl.pallas_call(
        functools.partial(_na_kernel, grid_rows=grid_rows),
        grid=(bsz, 4, n_blk),
        in_specs=[pl.BlockSpec((None, None, tq, LANES), lambda b, p, j: (b, P_BQ + p, j, 0)),
                  pl.BlockSpec((None, None, s_len, LANES), lambda b, p, j: (b, P_BK + p, 0, 0)),
                  pl.BlockSpec((None, None, s_len, LANES), lambda b, p, j: (b, P_BV + p, 0, 0)),
                  pl.BlockSpec((None, None, c_len, LANES), lambda b, p, j: (b, P_BK + p, 0, 0)),
                  pl.BlockSpec((None, None, c_len, LANES), lambda b, p, j: (b, P_BV + p, 0, 0)),
                  pl.BlockSpec((None, 2, tq, n_keys), lambda b, p, j: (cfg(j), p, 0, 0))],
        out_specs=pl.BlockSpec((None, None, tq, LANES), lambda b, p, j: (b, p, j, 0)),
        out_shape=jax.ShapeDtypeStruct((bsz, 4, s_len, LANES), BF16),
        compiler_params=_params("parallel", "parallel", "parallel"),
    )(qkv_x, qkv_x, qkv_x, qkv_c, qkv_c, bias_tabs)


def _swa_kernel(q_ref, kx_ref, vx_ref, kc_ref, vc_ref, sink_ref, o_ref):
    n_qp, tq, _ = q_ref.shape
    n_heads = 2 * n_qp
    s_len = kx_ref.shape[0]
    n_keys = tq + 2 * SWA_WINDOW
    q0 = pl.program_id(2) * tq
    start = pl.multiple_of(jnp.clip(q0 - SWA_WINDOW, 0, s_len - n_keys), SWA_WINDOW)
    k_slab = kx_ref[pl.ds(start, n_keys), :]
    v_slab = vx_ref[pl.ds(start, n_keys), :]
    qs = _stack_heads(q_ref, n_qp)
    qpos = q0 + lax.broadcasted_iota(jnp.int32, (tq, n_keys), 0)
    kpos = start + lax.broadcasted_iota(jnp.int32, (tq, n_keys), 1)
    band = jnp.where(jnp.abs(kpos - qpos) <= SWA_WINDOW, 0.0, NEG).astype(F32)
    s_loc = (_dot_nt(qs, k_slab).reshape(n_heads, tq, n_keys) + band[None]).reshape(n_heads * tq, n_keys)
    s_ctx = _dot_nt(qs, kc_ref[...])
    sink = _sink_column(sink_ref, pl.program_id(1), n_heads, tq)
    m = jnp.maximum(jnp.maximum(jnp.max(s_loc, axis=-1, keepdims=True), jnp.max(s_ctx, axis=-1, keepdims=True)),
                    sink)
    p_loc = jnp.exp(s_loc - m)
    p_ctx = jnp.exp(s_ctx - m)
    l = jnp.sum(p_loc, axis=-1, keepdims=True) + jnp.sum(p_ctx, axis=-1, keepdims=True) + jnp.exp(sink - m)
    o = (_dot(p_loc.astype(BF16), v_slab) + _dot(p_ctx.astype(BF16), vc_ref[...])) / l
    _store_heads(o_ref, o, n_qp, tq)


def _swa_attn(qkv_x, qkv_c, sink, *, tq=256):
    bsz, _, s_len, _ = qkv_x.shape
    c_len = qkv_c.shape[2]
    n_qp = 2
    return pl.pallas_call(
        _swa_kernel,
        grid=(bsz, 2, s_len // tq),
        in_specs=[pl.BlockSpec((None, n_qp, tq, LANES), lambda b, g, i: (b, P_DQ // n_qp + g, i, 0)),
                  pl.BlockSpec((None, None, s_len, LANES), lambda b, g, i: (b, P_DK + g, 0, 0)),
                  pl.BlockSpec((None, None, s_len, LANES), lambda b, g, i: (b, P_DV + g, 0, 0)),
                  pl.BlockSpec((None, None, c_len, LANES), lambda b, g, i: (b, P_DK + g, 0, 0)),
                  pl.BlockSpec((None, None, c_len, LANES), lambda b, g, i: (b, P_DV + g, 0, 0)),
                  pl.BlockSpec(memory_space=pltpu.SMEM)],
        out_specs=pl.BlockSpec((None, n_qp, tq, LANES), lambda b, g, i: (b, g, i, 0)),
        out_shape=jax.ShapeDtypeStruct((bsz, 2 * n_qp, s_len, LANES), BF16),
        compiler_params=_params("parallel", "parallel", "parallel"),
    )(qkv_x, qkv_x, qkv_x, qkv_c, qkv_c, sink)


def _merge_kernel(h_ref, mod_ref, g_ref, ya_ref, yb_ref, yc_ref, yd_ref, wgate_ref, wbr_ref, wout_ref, o_ref):
    h = h_ref[...]
    mod = mod_ref[...]
    g = g_ref[...]
    d = h.shape[1]
    u = (_rms(h) * g[2:3] * (1.0 + mod[4:5]) + mod[3:4]).astype(BF16)
    acc = None
    for i, y_ref in enumerate((ya_ref, yb_ref, yc_ref, yd_ref)):
        y = jnp.concatenate([y_ref[p] for p in range(y_ref.shape[0])], axis=-1)
        term = _sigmoid(_dot(u, wgate_ref[:, i * d:(i + 1) * d])) * _dot(y, wbr_ref[i])
        acc = term if acc is None else acc + term
    out = _dot(acc.astype(BF16), wout_ref[...])
    o_ref[...] = h + mod[5:6] * (_rms(out) * g[3:4])


def _merge(h, mod, gains, ys, w_gate, w_branch, w_out, *, mod_row, tm):
    bsz, t, d = h.shape
    n_yp = ys[0].shape[1]
    y_spec = pl.BlockSpec((None, n_yp, tm, LANES), lambda b, i: (b, 0, i, 0))
    return pl.pallas_call(
        _merge_kernel,
        grid=(bsz, t // tm),
        in_specs=[pl.BlockSpec((None, tm, d), lambda b, i: (b, i, 0)),
                  pl.BlockSpec((None, N_MOD, d), lambda b, i: (mod_row(b), 0, 0)),
                  _const_spec((6, d)), y_spec, y_spec, y_spec, y_spec,
                  _const_spec(w_gate.shape), _const_spec(w_branch.shape), _const_spec(w_out.shape)],
        out_specs=pl.BlockSpec((None, tm, d), lambda b, i: (b, i, 0)),
        out_shape=jax.ShapeDtypeStruct((bsz, t, d), F32),
        compiler_params=_params("parallel", "parallel"),
    )(h, mod, gains, *ys, w_gate, w_branch, w_out)


def _rope_tables(n_tok):
    t = jnp.arange(n_tok, dtype=jnp.int32)
    row = (t // GRID_W).astype(F32)
    col = (t % GRID_W).astype(F32)
    n_freq = HEAD_DIM // 4
    inv_freq = ROPE_THETA ** (-jnp.arange(n_freq, dtype=F32) / n_freq)
    lane = jnp.arange(LANES, dtype=jnp.int32)
    axis = (lane % HEAD_DIM) // (HEAD_DIM // 2)
    freq = inv_freq[lane % n_freq]
    ang = jnp.where(axis[None, :] == 0, row[:, None], col[:, None]) * freq[None, :]
    first = (lane % (HEAD_DIM // 2)) < n_freq
    cos, sin = jnp.cos(ang), jnp.sin(ang)
    return cos, jnp.where(first[None, :], -sin, 0.0), jnp.where(first[None, :], 0.0, sin)


def _dup_heads(w):
    h0, h1 = w[:, :HEAD_DIM], w[:, HEAD_DIM:]
    return jnp.concatenate([h0, h0, h1, h1], axis=1)


def _qkv_weight(w_in):
    o = 0
    cols = []
    for name, width in (("a", 1536), ("b", 1536)):
        cols.append(w_in[:, o:o + width])
        o += width
    for _ in ("c", "d"):
        cols += [w_in[:, o:o + 512], _dup_heads(w_in[:, o + 512:o + 640]), _dup_heads(w_in[:, o + 640:o + 768])]
        o += 768
    return jnp.concatenate(cols, axis=1).astype(BF16), o


def kernel(x, c, ctx, c_ctx, w_mod, b_mod, norm_gain, ffn_w_gate, ffn_w_up, ffn_w_down, w_in, w_branch,
           w_out, diff_lambda, diff_subln, na_bias, qk_norm, sink):
    bsz, s_len, d = x.shape
    c_len = ctx.shape[1]
    depth = w_mod.shape[0]
    tm = min(512, s_len)
    cond_t = jnp.zeros((d, 8), F32).at[:, :bsz].set(c.T).at[:, bsz].set(c_ctx)
    rope_tabs = _rope_tables(s_len)
    grp = jnp.arange(LANES, dtype=jnp.int32) // HEAD_DIM
    gmat = jnp.where(grp[:, None] == grp[None, :], 1.0 / HEAD_DIM, 0.0).astype(BF16)
    x_row = lambda b: b
    c_row = lambda b: bsz

    h_x, h_c = x, ctx
    for l in range(depth):
        last = l == depth - 1
        lam_init = 0.8 - 0.6 * math.exp(-0.3 * l)
        mod = _modulation(cond_t, w_mod[l], b_mod[l], bsz + 1).reshape(bsz + 1, N_MOD, d)
        g = norm_gain[l]
        wg, wu, wd = ffn_w_gate[l].astype(BF16), ffn_w_up[l].astype(BF16), ffn_w_down[l].astype(BF16)
        w_qkv, gate_off = _qkv_weight(w_in[l])
        w_gate = w_in[l][:, gate_off:].astype(BF16)
        w_br = w_branch[l].astype(BF16)
        w_o = w_out[l].astype(BF16)
        qk_gain = jnp.tile(qk_norm[l], (1, 2))

        h_x = _ffn(h_x, mod, g, wg[0], wu[0], wd[0], k=0, mod_row=x_row, tm=tm)
        h_c = _ffn(h_c, mod, g, wg[0], wu[0], wd[0], k=0, mod_row=c_row, tm=c_len)

        qkv_x = _inproj(h_x, mod, g, w_qkv, qk_gain, gmat, rope_tabs, mod_row=x_row, tm=tm)
        qkv_c = _inproj(h_c, mod, g, w_qkv, qk_gain, gmat, None, mod_row=c_row, tm=c_len)

        ya = _full_attn(qkv_x, P_AQ, qkv_x, qkv_c, P_AK, P_AV, n_qp=1, tq=min(512, s_len), n_groups=4,
                        diff_lambda=diff_lambda[l], diff_subln=diff_subln[l], lam_init=lam_init)
        yb = _na_attn(qkv_x, qkv_c, _na_bias_tables(na_bias[l], s_len // GRID_W))
        yc = _full_attn(qkv_x, P_CQ, qkv_x, qkv_c, P_CK, P_CV, n_qp=2, tq=min(256, s_len), n_groups=2)
        yd = _swa_attn(qkv_x, qkv_c, sink[l])
        h_x = _merge(h_x, mod, g, (ya, yb, yc, yd), w_gate, w_br, w_o, mod_row=x_row, tm=tm)
        h_x = _ffn(h_x, mod, g, wg[1], wu[1], wd[1], k=2, mod_row=x_row, tm=tm)

        if not last:
            ya = _full_attn(qkv_c, P_AQ, None, qkv_c, P_AK, P_AV, n_qp=1, tq=c_len, n_groups=4,
                            diff_lambda=diff_lambda[l], diff_subln=diff_subln[l], lam_init=lam_init)
            yb = _full_attn(qkv_c, P_BQ, None, qkv_c, P_BK, P_BV, n_qp=1, tq=c_len, n_groups=4)
            yc = _full_attn(qkv_c, P_CQ, None, qkv_c, P_CK, P_CV, n_qp=2, tq=c_len, n_groups=2)
            yd = _full_attn(qkv_c, P_DQ, None, qkv_c, P_DK, P_DV, n_qp=2, tq=c_len, n_groups=2, sink=sink[l])
            h_c = _merge(h_c, mod, g, (ya, yb, yc, yd), w_gate, w_br, w_o, mod_row=c_row, tm=c_len)
            h_c = _ffn(h_c, mod, g, wg[1], wu[1], wd[1], k=2, mod_row=c_row, tm=c_len)
    return h_x
```

```python
import functools
import math

import jax
import jax.numpy as jnp
from jax import lax
from jax.experimental import pallas as pl
from jax.experimental.pallas import tpu as pltpu

HEAD_DIM = 64
LANES = 128
GRID_W = 64
NA_KH = 8
NA_KW = 16
NA_QROWS = 4
NA_SLAB = NA_QROWS + NA_KH
SWA_WINDOW = 128
KV_CHUNK = 1024
ATTN_COL_BLOCK = 512
FFN_RES_WEIGHT = 0.5
ROPE_THETA = 10000.0
NORM_EPS = 1e-6
N_MOD = 9
NEG = -1e30
VMEM_LIMIT = 48 * 1024 * 1024
BF16 = jnp.bfloat16
F32 = jnp.float32

P_AQ, P_AK, P_AV = 0, 4, 8
P_BQ, P_BK, P_BV = 12, 16, 20
P_CQ, P_CK, P_CV = 24, 28, 30
P_DQ, P_DK, P_DV = 32, 36, 38
N_PAIRS = 40
COLS_X = tuple(range(P_AQ, P_AQ + 4)) + tuple(range(P_AV, P_AV + 4)) + tuple(range(P_CQ, P_CQ + 4)) + (P_CV, P_CV + 1)
COLS_C = (COLS_X[:8] + tuple(range(P_BQ, P_BQ + 4)) + tuple(range(P_BV, P_BV + 4)) + COLS_X[8:]
          + tuple(range(P_DQ, P_DQ + 4)) + (P_DV, P_DV + 1))
SEGMENTS = (
    (P_AQ, 4, None, True, True), (P_AK, 4, None, True, False), (P_AV, 4, None, False, False),
    (P_BQ, 4, None, False, True), (P_BK, 4, None, False, False), (P_BV, 4, None, False, False),
    (P_CQ, 4, 0, True, True), (P_CK, 2, 1, True, False), (P_CV, 2, None, False, False),
    (P_DQ, 4, None, True, True), (P_DK, 2, None, True, False), (P_DV, 2, None, False, False),
)
LOG2_E = math.log2(math.e)
Q_SCALE = HEAD_DIM ** -0.5 * LOG2_E


def _const_spec(shape):
    zeros = (0,) * len(shape)
    return pl.BlockSpec(shape, lambda *_: zeros, pipeline_mode=pl.Buffered(1))


def _params(*sem):
    return pltpu.CompilerParams(dimension_semantics=sem, vmem_limit_bytes=VMEM_LIMIT)


def _rms(x):
    return x * lax.rsqrt(jnp.mean(x * x, axis=-1, keepdims=True) + NORM_EPS)


def _sigmoid(x):
    return 1.0 / (1.0 + jnp.exp(-x))


def _dot(a, b):
    return jnp.dot(a, b, preferred_element_type=F32)


def _dot_nt(a, b):
    return lax.dot_general(a, b, (((1,), (1,)), ((), ())), preferred_element_type=F32)


def _mod_kernel(ct_ref, w_ref, b_ref, o_ref, *, n_rows):
    ct = ct_ref[...]
    s = ct * _sigmoid(ct)
    w = w_ref[...]
    for r in range(n_rows):
        o_ref[r:r + 1, :] = jnp.sum(w * s[:, r:r + 1], axis=0, keepdims=True) + b_ref[...]


def _modulation(cond_t, w_mod, b_mod, n_rows):
    d, n = w_mod.shape
    tn = 1024
    return pl.pallas_call(
        functools.partial(_mod_kernel, n_rows=n_rows),
        grid=(n // tn,),
        in_specs=[pl.BlockSpec((d, 8), lambda j: (0, 0)),
                  pl.BlockSpec((d, tn), lambda j: (0, j)),
                  pl.BlockSpec((1, tn), lambda j: (0, j))],
        out_specs=pl.BlockSpec((n_rows, tn), lambda j: (0, j)),
        out_shape=jax.ShapeDtypeStruct((n_rows, n), F32),
        compiler_params=_params("parallel"),
    )(cond_t, w_mod, b_mod.reshape(1, n))


def _ffn_kernel(h_ref, mod_ref, g_ref, wg_ref, wu_ref, wd_ref, o_ref, *, k, n_chunks):
    h = h_ref[...]
    mod = mod_ref[...]
    g = g_ref[...]
    u = (_rms(h) * g[2 * k:2 * k + 1] * (1.0 + mod[3 * k + 1:3 * k + 2]) + mod[3 * k:3 * k + 1]).astype(BF16)
    cf = wg_ref.shape[1] // n_chunks
    y = None
    for c in range(n_chunks):
        gate = _dot(u, wg_ref[:, c * cf:(c + 1) * cf])
        up = _dot(u, wu_ref[:, c * cf:(c + 1) * cf])
        act = (gate * _sigmoid(gate) * up).astype(BF16)
        part = _dot(act, wd_ref[c * cf:(c + 1) * cf, :])
        y = part if y is None else y + part
    o_ref[...] = h + FFN_RES_WEIGHT * mod[3 * k + 2:3 * k + 3] * (_rms(y) * g[2 * k + 1:2 * k + 2])


def _ffn(h, mod, gains, wg, wu, wd, *, k, mod_row, tm, n_tiles=None):
    bsz, t, d = h.shape
    n_tiles = t // tm if n_tiles is None else n_tiles
    f = wg.shape[1]
    return pl.pallas_call(
        functools.partial(_ffn_kernel, k=k, n_chunks=2),
        grid=(bsz, n_tiles),
        in_specs=[pl.BlockSpec((None, tm, d), lambda b, i: (b, i, 0)),
                  pl.BlockSpec((None, N_MOD, d), lambda b, i: (mod_row(b), 0, 0)),
                  _const_spec((6, d)), _const_spec((d, f)), _const_spec((d, f)), _const_spec((f, d))],
        out_specs=pl.BlockSpec((None, tm, d), lambda b, i: (b, i, 0)),
        out_shape=jax.ShapeDtypeStruct((bsz, n_tiles * tm, d), F32),
        compiler_params=_params("parallel", "parallel"),
    )(h, mod, gains, wg, wu, wd)


def _inproj_kernel(*refs, rope, cols):
    if rope:
        h_ref, mod_ref, g_ref, w_ref, qkg_ref, gm_ref, cos_ref, sa_ref, sb_ref, o_ref, t_ref = refs
    else:
        h_ref, mod_ref, g_ref, w_ref, qkg_ref, gm_ref, o_ref, t_ref = refs
    h = h_ref[...]
    mod = mod_ref[...]
    u = (_rms(h) * g_ref[2:3, :] * (1.0 + mod[4:5]) + mod[3:4]).astype(BF16)
    for p0, n_p, norm_row, rotary, scaled in SEGMENTS:
        r = _dot(u, w_ref[:, p0 * LANES:(p0 + n_p) * LANES])
        for j in range(n_p):
            x = r[:, j * LANES:(j + 1) * LANES]
            if norm_row is not None:
                xx = x * x
                hi = xx.astype(BF16)
                lo = (xx - hi.astype(F32)).astype(BF16)
                ms = _dot(hi, gm_ref[...]) + _dot(lo, gm_ref[...])
                x = x * lax.rsqrt(ms + NORM_EPS) * qkg_ref[norm_row:norm_row + 1, :]
            if rotary and rope:
                x = (x * cos_ref[...] + pltpu.roll(x, LANES - 16, 1) * sa_ref[...]
                     + pltpu.roll(x, 16, 1) * sb_ref[...])
            if scaled:
                x = x * Q_SCALE
            o_ref[p0 + j] = x.astype(BF16)
            if p0 + j in cols:
                t_ref[cols.index(p0 + j)] = x.T.astype(BF16)


def _inproj(h, mod, gains, w_qkv, qk_gain, gmat, rope_tabs, *, mod_row, tm, cols):
    bsz, t, d = h.shape
    rope = rope_tabs is not None
    chunk = min(KV_CHUNK, t)
    per_chunk = chunk // tm
    in_specs = [pl.BlockSpec((None, tm, d), lambda b, i: (b, i, 0)),
                pl.BlockSpec((None, N_MOD, d), lambda b, i: (mod_row(b), 0, 0)),
                _const_spec((6, d)), _const_spec(w_qkv.shape), _const_spec((2, LANES)),
                _const_spec((LANES, LANES))]
    args = [h, mod, gains, w_qkv, qk_gain, gmat]
    if rope:
        in_specs += [pl.BlockSpec((tm, LANES), lambda b, i: (i, 0))] * 3
        args += list(rope_tabs)
    return pl.pallas_call(
        functools.partial(_inproj_kernel, rope=rope, cols=cols),
        grid=(bsz, t // tm),
        in_specs=in_specs,
        out_specs=[pl.BlockSpec((None, N_PAIRS, tm, LANES), lambda b, i: (b, 0, i, 0)),
                   pl.BlockSpec((None, len(cols), None, LANES, tm),
                                lambda b, i: (b, 0, i // per_chunk, 0, i % per_chunk))],
        out_shape=[jax.ShapeDtypeStruct((bsz, N_PAIRS, t, LANES), BF16),
                   jax.ShapeDtypeStruct((bsz, len(cols), t // chunk, LANES, chunk), BF16)],
        compiler_params=_params("parallel", "parallel"),
    )(*args)


def _half_masks(shape):
    lane = lax.broadcasted_iota(jnp.int32, shape, len(shape) - 1)
    return lane < HEAD_DIM


def _stack_heads(q_ref, n_qp):
    parts = []
    for p in range(n_qp):
        q = q_ref[p]
        lo = _half_masks(q.shape)
        parts += [jnp.where(lo, q, jnp.zeros_like(q)), jnp.where(lo, jnp.zeros_like(q), q)]
    return jnp.concatenate(parts, axis=0)


def _store_heads(o_ref, o, n_qp, tq):
    for p in range(n_qp):
        even = o[(2 * p) * tq:(2 * p + 1) * tq]
        odd = o[(2 * p + 1) * tq:(2 * p + 2) * tq]
        o_ref[p] = jnp.where(_half_masks(even.shape), even, odd).astype(o_ref.dtype)


def _sink_column(sink_ref, kv_head, n_heads, tq):
    row = lax.broadcasted_iota(jnp.int32, (n_heads * tq, 1), 0)
    col = jnp.full((n_heads * tq, 1), sink_ref[kv_head * n_heads], F32)
    for gq in range(1, n_heads):
        col = jnp.where(row >= gq * tq, sink_ref[kv_head * n_heads + gq], col)
    return col * LOG2_E


def _stack_heads_t(qt_ref, n_qp):
    parts = []
    for p in range(n_qp):
        qt = qt_ref[p]
        top = lax.broadcasted_iota(jnp.int32, qt.shape, 0) < HEAD_DIM
        parts += [jnp.where(top, qt, jnp.zeros_like(qt)), jnp.where(top, jnp.zeros_like(qt), qt)]
    return jnp.concatenate(parts, axis=1)


def _sink_row(sink_ref, kv_head, n_heads, tq):
    col = lax.broadcasted_iota(jnp.int32, (1, n_heads * tq), 1)
    row = jnp.full((1, n_heads * tq), sink_ref[kv_head * n_heads], F32)
    for gq in range(1, n_heads):
        row = jnp.where(col >= gq * tq, sink_ref[kv_head * n_heads + gq], row)
    return row * LOG2_E


def _full_attn_kernel(*refs, n_qp, use_x, use_sink, diff, col_blk, lam_init):
    refs = list(refs)
    qt_ref = refs.pop(0)
    kx_ref, vxt_ref = (refs.pop(0), refs.pop(0)) if use_x else (None, None)
    kc_ref, vct_ref = refs.pop(0), refs.pop(0)
    sink_ref = refs.pop(0) if use_sink else None
    dl_ref, sub_ref = (refs.pop(0), refs.pop(0)) if diff else (None, None)
    o_ref, m_ref, l_ref, acc_ref = refs[:4]
    sa_ref, sb_ref = refs[4:] if use_x else (None, None)
    tq = qt_ref.shape[2]
    n_heads = 2 * n_qp
    qs_t = _stack_heads_t(qt_ref, n_qp)

    if use_sink:
        m_ref[...] = _sink_row(sink_ref, pl.program_id(1), n_heads, tq)
        l_ref[...] = jnp.ones(l_ref.shape, F32)
    else:
        m_ref[...] = jnp.full(m_ref.shape, NEG, F32)
        l_ref[...] = jnp.zeros(l_ref.shape, F32)
    acc_ref[...] = jnp.zeros(acc_ref.shape, F32)

    n_blk = qs_t.shape[1] // col_blk

    def blk(c):
        return slice(c * col_blk, (c + 1) * col_blk)

    def scores(k, c):
        return _dot(k, qs_t[:, blk(c)])

    def softmax_pv(s, vt, c):
        m_prev = m_ref[:, blk(c)]
        m_new = jnp.maximum(m_prev, jnp.max(s, axis=0, keepdims=True))
        alpha = jnp.exp2(m_prev - m_new)
        p = jnp.exp2(s - m_new)
        l_ref[:, blk(c)] = alpha * l_ref[:, blk(c)] + jnp.sum(p, axis=0, keepdims=True)
        m_ref[:, blk(c)] = m_new
        acc_ref[:, blk(c)] = alpha * acc_ref[:, blk(c)] + _dot(vt, p.astype(BF16))

    if not use_x:
        for c in range(n_blk):
            softmax_pv(scores(kc_ref[...], c), vct_ref[...], c)
    else:
        n_chunk, _, tk = vxt_ref.shape

        def k_chunk(j):
            return kx_ref[pl.ds(pl.multiple_of(j * tk, tk), tk), :]

        def half_step(cur_ref, nxt_ref, j):
            k_next = k_chunk(jnp.minimum(j + 1, n_chunk - 1))
            vt = vxt_ref[j]
            for c in range(n_blk):
                nxt_ref[:, blk(c)] = scores(k_next, c)
                softmax_pv(cur_ref[:, blk(c)], vt, c)

        k0 = k_chunk(0)
        for c in range(n_blk):
            sa_ref[:, blk(c)] = scores(k0, c)
            softmax_pv(scores(kc_ref[...], c), vct_ref[...], c)

        def body(jj, carry):
            half_step(sa_ref, sb_ref, 2 * jj)
            half_step(sb_ref, sa_ref, 2 * jj + 1)
            return carry
        lax.fori_loop(0, n_chunk // 2, body, 0)
        if n_chunk % 2:
            half_step(sa_ref, sb_ref, n_chunk - 1)

    o_t = acc_ref[...] / l_ref[...]
    if diff:
        dl = dl_ref[...]
        lam = (jnp.exp(jnp.sum(dl[0:1] * dl[1:2], axis=-1, keepdims=True))
               - jnp.exp(jnp.sum(dl[2:3] * dl[3:4], axis=-1, keepdims=True)) + lam_init)
        d = (o_t[:, :tq] - lam * o_t[:, tq:]).T
        o_ref[0] = (_rms(d) * sub_ref[...] * (1.0 - lam_init)).astype(o_ref.dtype)
    else:
        top = lax.broadcasted_iota(jnp.int32, (LANES, tq), 0) < HEAD_DIM
        for p in range(n_qp):
            even = o_t[:, (2 * p) * tq:(2 * p + 1) * tq]
            odd = o_t[:, (2 * p + 1) * tq:(2 * p + 2) * tq]
            o_ref[p] = jnp.where(top, even, odd).T.astype(o_ref.dtype)


def _full_attn(q_cols, q_col0, rows_x, cols_x, rows_c, cols_c, k_pair0, vx_col0, vc_col0, *, n_qp, tq,
               n_groups, sink=None, diff_lambda=None, diff_subln=None, lam_init=0.0):
    bsz, _, n_qchunk, _, q_chunk = q_cols.shape
    t_q = n_qchunk * q_chunk
    per_chunk = q_chunk // tq
    c_len = rows_c.shape[2]
    use_x = rows_x is not None
    diff = diff_lambda is not None
    m_cols = 2 * n_qp * tq
    qp_blk = q_col0 // n_qp
    in_specs = [pl.BlockSpec((None, n_qp, None, LANES, tq),
                             lambda b, g, i: (b, qp_blk + g, i // per_chunk, 0, i % per_chunk))]
    args = [q_cols]
    if use_x:
        s_len = rows_x.shape[2]
        n_chunk, tk = cols_x.shape[2], cols_x.shape[4]
        in_specs += [pl.BlockSpec((None, None, s_len, LANES), lambda b, g, i: (b, k_pair0 + g, 0, 0)),
                     pl.BlockSpec((None, None, n_chunk, LANES, tk), lambda b, g, i: (b, vx_col0 + g, 0, 0, 0))]
        args += [rows_x, cols_x]
    in_specs += [pl.BlockSpec((None, None, c_len, LANES), lambda b, g, i: (b, k_pair0 + g, 0, 0)),
                 pl.BlockSpec((None, None, None, LANES, c_len), lambda b, g, i: (b, vc_col0 + g, 0, 0, 0))]
    args += [rows_c, cols_c]
    if sink is not None:
        in_specs.append(pl.BlockSpec(memory_space=pltpu.SMEM))
        args.append(sink)
    if diff:
        in_specs += [_const_spec((4, HEAD_DIM)), _const_spec((1, LANES))]
        args += [diff_lambda, diff_subln.reshape(1, LANES)]
    return pl.pallas_call(
        functools.partial(_full_attn_kernel, n_qp=n_qp, use_x=use_x, use_sink=sink is not None, diff=diff,
                          col_blk=min(ATTN_COL_BLOCK, m_cols), lam_init=lam_init),
        grid=(bsz, n_groups, t_q // tq),
        in_specs=in_specs,
        out_specs=pl.BlockSpec((None, n_qp, tq, LANES), lambda b, g, i: (b, g, i, 0)),
        out_shape=jax.ShapeDtypeStruct((bsz, n_groups * n_qp, t_q, LANES), BF16),
        scratch_shapes=[pltpu.VMEM((1, m_cols), F32), pltpu.VMEM((1, m_cols), F32),
                        pltpu.VMEM((LANES, m_cols), F32)]
        + ([pltpu.VMEM((cols_x.shape[4], m_cols), F32)] * 2 if use_x else []),
        compiler_params=_params("parallel", "parallel", "parallel"),
    )(*args)


def _na_kernel(q_ref, kx_ref, vx_ref, kc_ref, vc_ref, bias_ref, o_ref, *, grid_rows):
    j = pl.program_id(2)
    tq = q_ref.shape[0]
    n_keys = NA_SLAB * GRID_W
    row0 = jnp.clip(NA_QROWS * j - NA_KH // 2, 0, grid_rows - NA_SLAB)
    start = pl.multiple_of(row0 * GRID_W, GRID_W)
    k_slab = kx_ref[pl.ds(start, n_keys), :]
    v_slab = vx_ref[pl.ds(start, n_keys), :]
    q = q_ref[...]
    lo = _half_masks(q.shape)
    qs = jnp.concatenate([jnp.where(lo, q, jnp.zeros_like(q)), jnp.where(lo, jnp.zeros_like(q), q)], axis=0)
    s_loc = _dot_nt(qs, k_slab) + bias_ref[...].reshape(2 * tq, n_keys)
    s_ctx = _dot_nt(qs, kc_ref[...])
    m = jnp.maximum(jnp.max(s_loc, axis=-1, keepdims=True), jnp.max(s_ctx, axis=-1, keepdims=True))
    p_loc = jnp.exp2(s_loc - m)
    p_ctx = jnp.exp2(s_ctx - m)
    l = jnp.sum(p_loc, axis=-1, keepdims=True) + jnp.sum(p_ctx, axis=-1, keepdims=True)
    o = (_dot(p_loc.astype(BF16), v_slab) + _dot(p_ctx.astype(BF16), vc_ref[...])) / l
    o_ref[...] = jnp.where(lo, o[:tq], o[tq:]).astype(o_ref.dtype)


def _na_bias_tables(na_bias, grid_rows):
    n_heads, n_dr, n_dc = na_bias.shape
    n_blk = grid_rows // NA_QROWS
    plan = []
    for j in (0, 1, n_blk - 1):
        row0 = min(max(NA_QROWS * j - NA_KH // 2, 0), grid_rows - NA_SLAB)
        rows = []
        for i in range(NA_QROWS):
            r = NA_QROWS * j + i
            r0 = min(max(r - NA_KH // 2, 0), grid_rows - NA_KH)
            rows.append(tuple((row0 + t - r + NA_KH - 1) if r0 <= row0 + t < r0 + NA_KH else None
                              for t in range(NA_SLAB)))
        plan.append(tuple(rows))
    padded = jnp.pad(na_bias, ((0, 0), (0, 16 - n_dr), (0, LANES - n_dc)))
    return pl.pallas_call(
        functools.partial(_na_bias_kernel, plan=tuple(plan)),
        grid=(n_heads,),
        in_specs=[pl.BlockSpec((None, 16, LANES), lambda h: (h, 0, 0))],
        out_specs=pl.BlockSpec((3, None, NA_QROWS * GRID_W, NA_SLAB * GRID_W), lambda h: (0, h, 0, 0)),
        out_shape=jax.ShapeDtypeStruct((3, n_heads, NA_QROWS * GRID_W, NA_SLAB * GRID_W), F32),
        compiler_params=_params("parallel"),
    )(padded)


def _na_bias_kernel(b_ref, o_ref, *, plan):
    shape = (GRID_W, LANES)
    lane = lax.broadcasted_iota(jnp.int32, shape, 1)
    cq = lax.broadcasted_iota(jnp.int32, shape, 0)
    kc = lane % GRID_W
    c0 = jnp.clip(cq - NA_KW // 2, 0, GRID_W - NA_KW)
    col_bias = jnp.where(kc >= c0, jnp.where(kc < c0 + NA_KW, 0.0, NEG), NEG).astype(F32)
    left_half = lane < GRID_W
    neg = jnp.full(shape, NEG, F32)
    used = sorted({dr for cfg in plan for row in cfg for dr in row if dr is not None})
    shift = LANES - (NA_KW - 1)
    tile_lo, tile_hi = {}, {}
    for dr in used:
        row = jnp.broadcast_to(b_ref[dr:dr + 1, :] * LOG2_E, shape)
        tile_lo[dr] = pltpu.roll(row, shift, 1, stride=1, stride_axis=0)
        tile_hi[dr] = pltpu.roll(row, (shift + GRID_W) % LANES, 1, stride=1, stride_axis=0)
    for cfg, cfg_rows in enumerate(plan):
        for i, row in enumerate(cfg_rows):
            for u in range(NA_SLAB // 2):
                left = neg if row[2 * u] is None else tile_lo[row[2 * u]]
                right = neg if row[2 * u + 1] is None else tile_hi[row[2 * u + 1]]
                blk = jnp.where(col_bias < 0.0, neg, jnp.where(left_half, left, right))
                o_ref[cfg, i * GRID_W:(i + 1) * GRID_W, u * LANES:(u + 1) * LANES] = blk


def _na_attn(qkv_x, qkv_c, bias_tabs):
    bsz, _, s_len, _ = qkv_x.shape
    c_len = qkv_c.shape[2]
    grid_rows = s_len // GRID_W
    n_blk = grid_rows // NA_QROWS
    tq = NA_QROWS * GRID_W
    n_keys = NA_SLAB * GRID_W

    def cfg(j):
        return jnp.where(j == 0, 0, jnp.where(j == n_blk - 1, 2, 1))

    return pl.pallas_call(
        functools.partial(_na_kernel, grid_rows=grid_rows),
        grid=(bsz, 4, n_blk),
        in_specs=[pl.BlockSpec((None, None, tq, LANES), lambda b, p, j: (b, P_BQ + p, j, 0)),
                  pl.BlockSpec((None, None, s_len, LANES), lambda b, p, j: (b, P_BK + p, 0, 0)),
                  pl.BlockSpec((None, None, s_len, LANES), lambda b, p, j: (b, P_BV + p, 0, 0)),
                  pl.BlockSpec((None, None, c_len, LANES), lambda b, p, j: (b, P_BK + p, 0, 0)),
                  pl.BlockSpec((None, None, c_len, LANES), lambda b, p, j: (b, P_BV + p, 0, 0)),
                  pl.BlockSpec((None, 2, tq, n_keys), lambda b, p, j: (cfg(j), p, 0, 0))],
        out_specs=pl.BlockSpec((None, None, tq, LANES), lambda b, p, j: (b, p, j, 0)),
        out_shape=jax.ShapeDtypeStruct((bsz, 4, s_len, LANES), BF16),
        compiler_params=_params("parallel", "parallel", "parallel"),
    )(qkv_x, qkv_x, qkv_x, qkv_c, qkv_c, bias_tabs)


def _swa_kernel(q_ref, kx_ref, vx_ref, kc_ref, vc_ref, sink_ref, o_ref):
    n_qp, tq, _ = q_ref.shape
    n_heads = 2 * n_qp
    s_len = kx_ref.shape[0]
    n_keys = tq + 2 * SWA_WINDOW
    q0 = pl.program_id(2) * tq
    start = pl.multiple_of(jnp.clip(q0 - SWA_WINDOW, 0, s_len - n_keys), SWA_WINDOW)
    k_slab = kx_ref[pl.ds(start, n_keys), :]
    v_slab = vx_ref[pl.ds(start, n_keys), :]
    qs = _stack_heads(q_ref, n_qp)
    qpos = q0 + lax.broadcasted_iota(jnp.int32, (tq, n_keys), 0)
    kpos = start + lax.broadcasted_iota(jnp.int32, (tq, n_keys), 1)
    band = jnp.where(jnp.abs(kpos - qpos) <= SWA_WINDOW, 0.0, NEG).astype(F32)
    s_loc = (_dot_nt(qs, k_slab).reshape(n_heads, tq, n_keys) + band[None]).reshape(n_heads * tq, n_keys)
    s_ctx = _dot_nt(qs, kc_ref[...])
    sink = _sink_column(sink_ref, pl.program_id(1), n_heads, tq)
    m = jnp.maximum(jnp.maximum(jnp.max(s_loc, axis=-1, keepdims=True), jnp.max(s_ctx, axis=-1, keepdims=True)),
                    sink)
    p_loc = jnp.exp2(s_loc - m)
    p_ctx = jnp.exp2(s_ctx - m)
    l = jnp.sum(p_loc, axis=-1, keepdims=True) + jnp.sum(p_ctx, axis=-1, keepdims=True) + jnp.exp2(sink - m)
    o = (_dot(p_loc.astype(BF16), v_slab) + _dot(p_ctx.astype(BF16), vc_ref[...])) / l
    _store_heads(o_ref, o, n_qp, tq)


def _swa_attn(qkv_x, qkv_c, sink, *, tq=256):
    bsz, _, s_len, _ = qkv_x.shape
    c_len = qkv_c.shape[2]
    n_qp = 2
    return pl.pallas_call(
        _swa_kernel,
        grid=(bsz, 2, s_len // tq),
        in_specs=[pl.BlockSpec((None, n_qp, tq, LANES), lambda b, g, i: (b, P_DQ // n_qp + g, i, 0)),
                  pl.BlockSpec((None, None, s_len, LANES), lambda b, g, i: (b, P_DK + g, 0, 0)),
                  pl.BlockSpec((None, None, s_len, LANES), lambda b, g, i: (b, P_DV + g, 0, 0)),
                  pl.BlockSpec((None, None, c_len, LANES), lambda b, g, i: (b, P_DK + g, 0, 0)),
                  pl.BlockSpec((None, None, c_len, LANES), lambda b, g, i: (b, P_DV + g, 0, 0)),
                  pl.BlockSpec(memory_space=pltpu.SMEM)],
        out_specs=pl.BlockSpec((None, n_qp, tq, LANES), lambda b, g, i: (b, g, i, 0)),
        out_shape=jax.ShapeDtypeStruct((bsz, 2 * n_qp, s_len, LANES), BF16),
        compiler_params=_params("parallel", "parallel", "parallel"),
    )(qkv_x, qkv_x, qkv_x, qkv_c, qkv_c, sink)


def _merge_kernel(h_ref, mod_ref, g_ref, ya_ref, yb_ref, yc_ref, yd_ref, wgate_ref, wbr_ref, wout_ref, o_ref):
    h = h_ref[...]
    mod = mod_ref[...]
    g = g_ref[...]
    d = h.shape[1]
    u = (_rms(h) * g[2:3] * (1.0 + mod[4:5]) + mod[3:4]).astype(BF16)
    acc = None
    for i, y_ref in enumerate((ya_ref, yb_ref, yc_ref, yd_ref)):
        y = jnp.concatenate([y_ref[p] for p in range(y_ref.shape[0])], axis=-1)
        term = _sigmoid(_dot(u, wgate_ref[:, i * d:(i + 1) * d])) * _dot(y, wbr_ref[i])
        acc = term if acc is None else acc + term
    out = _dot(acc.astype(BF16), wout_ref[...])
    o_ref[...] = h + mod[5:6] * (_rms(out) * g[3:4])


def _merge(h, mod, gains, ys, w_gate, w_branch, w_out, *, mod_row, tm):
    bsz, t, d = h.shape
    n_yp = ys[0].shape[1]
    y_spec = pl.BlockSpec((None, n_yp, tm, LANES), lambda b, i: (b, 0, i, 0))
    return pl.pallas_call(
        _merge_kernel,
        grid=(bsz, t // tm),
        in_specs=[pl.BlockSpec((None, tm, d), lambda b, i: (b, i, 0)),
                  pl.BlockSpec((None, N_MOD, d), lambda b, i: (mod_row(b), 0, 0)),
                  _const_spec((6, d)), y_spec, y_spec, y_spec, y_spec,
                  _const_spec(w_gate.shape), _const_spec(w_branch.shape), _const_spec(w_out.shape)],
        out_specs=pl.BlockSpec((None, tm, d), lambda b, i: (b, i, 0)),
        out_shape=jax.ShapeDtypeStruct((bsz, t, d), F32),
        compiler_params=_params("parallel", "parallel"),
    )(h, mod, gains, *ys, w_gate, w_branch, w_out)


def _rope_tables(n_tok):
    t = jnp.arange(n_tok, dtype=jnp.int32)
    row = (t // GRID_W).astype(F32)
    col = (t % GRID_W).astype(F32)
    n_freq = HEAD_DIM // 4
    inv_freq = ROPE_THETA ** (-jnp.arange(n_freq, dtype=F32) / n_freq)
    lane = jnp.arange(LANES, dtype=jnp.int32)
    axis = (lane % HEAD_DIM) // (HEAD_DIM // 2)
    freq = inv_freq[lane % n_freq]
    ang = jnp.where(axis[None, :] == 0, row[:, None], col[:, None]) * freq[None, :]
    first = (lane % (HEAD_DIM // 2)) < n_freq
    cos, sin = jnp.cos(ang), jnp.sin(ang)
    return cos, jnp.where(first[None, :], -sin, 0.0), jnp.where(first[None, :], 0.0, sin)


def _dup_heads(w):
    h0, h1 = w[:, :HEAD_DIM], w[:, HEAD_DIM:]
    return jnp.concatenate([h0, h0, h1, h1], axis=1)


def _qkv_weight(w_in):
    o = 0
    cols = []
    for name, width in (("a", 1536), ("b", 1536)):
        cols.append(w_in[:, o:o + width])
        o += width
    for _ in ("c", "d"):
        cols += [w_in[:, o:o + 512], _dup_heads(w_in[:, o + 512:o + 640]), _dup_heads(w_in[:, o + 640:o + 768])]
        o += 768
    return jnp.concatenate(cols, axis=1).astype(BF16), o


def kernel(x, c, ctx, c_ctx, w_mod, b_mod, norm_gain, ffn_w_gate, ffn_w_up, ffn_w_down, w_in, w_branch,
           w_out, diff_lambda, diff_subln, na_bias, qk_norm, sink):
    bsz, s_len, d = x.shape
    c_len = ctx.shape[1]
    depth = w_mod.shape[0]
    tm = min(512, s_len)
    cond_t = jnp.zeros((d, 8), F32).at[:, :bsz].set(c.T).at[:, bsz].set(c_ctx)
    rope_tabs = _rope_tables(s_len)
    grp = jnp.arange(LANES, dtype=jnp.int32) // HEAD_DIM
    gmat = jnp.where(grp[:, None] == grp[None, :], 1.0 / HEAD_DIM, 0.0).astype(BF16)
    x_row = lambda b: b
    c_row = lambda b: bsz

    h_x, h_c = x, ctx
    for l in range(depth):
        last = l == depth - 1
        lam_init = 0.8 - 0.6 * math.exp(-0.3 * l)
        mod = _modulation(cond_t, w_mod[l], b_mod[l], bsz + 1).reshape(bsz + 1, N_MOD, d)
        g = norm_gain[l]
        wg, wu, wd = ffn_w_gate[l].astype(BF16), ffn_w_up[l].astype(BF16), ffn_w_down[l].astype(BF16)
        w_qkv, gate_off = _qkv_weight(w_in[l])
        w_gate = w_in[l][:, gate_off:].astype(BF16)
        w_br = w_branch[l].astype(BF16)
        w_o = w_out[l].astype(BF16)
        qk_gain = jnp.tile(qk_norm[l], (1, 2))

        h_x = _ffn(h_x, mod, g, wg[0], wu[0], wd[0], k=0, mod_row=x_row, tm=tm)
        h_c = _ffn(h_c, mod, g, wg[0], wu[0], wd[0], k=0, mod_row=c_row, tm=c_len)

        qkv_x, colx = _inproj(h_x, mod, g, w_qkv, qk_gain, gmat, rope_tabs, mod_row=x_row, tm=tm, cols=COLS_X)
        qkv_c, colc = _inproj(h_c, mod, g, w_qkv, qk_gain, gmat, None, mod_row=c_row, tm=c_len, cols=COLS_C)
        cx, cc = COLS_X.index, COLS_C.index

        ya = _full_attn(colx, cx(P_AQ), qkv_x, colx, qkv_c, colc, P_AK, cx(P_AV), cc(P_AV), n_qp=1,
                        tq=min(512, s_len), n_groups=4,
                        diff_lambda=diff_lambda[l], diff_subln=diff_subln[l], lam_init=lam_init)
        yb = _na_attn(qkv_x, qkv_c, _na_bias_tables(na_bias[l], s_len // GRID_W))
        yc = _full_attn(colx, cx(P_CQ), qkv_x, colx, qkv_c, colc, P_CK, cx(P_CV), cc(P_CV), n_qp=2,
                        tq=min(256, s_len), n_groups=2)
        yd = _swa_attn(qkv_x, qkv_c, sink[l])
        h_x = _merge(h_x, mod, g, (ya, yb, yc, yd), w_gate, w_br, w_o, mod_row=x_row, tm=tm)
        h_x = _ffn(h_x, mod, g, wg[1], wu[1], wd[1], k=2, mod_row=x_row, tm=tm)

        if not last:
            ya = _full_attn(colc, cc(P_AQ), None, None, qkv_c, colc, P_AK, None, cc(P_AV), n_qp=1, tq=c_len,
                            n_groups=4, diff_lambda=diff_lambda[l], diff_subln=diff_subln[l], lam_init=lam_init)
            yb = _full_attn(colc, cc(P_BQ), None, None, qkv_c, colc, P_BK, None, cc(P_BV), n_qp=1, tq=c_len,
                            n_groups=4)
            yc = _full_attn(colc, cc(P_CQ), None, None, qkv_c, colc, P_CK, None, cc(P_CV), n_qp=2, tq=c_len,
                            n_groups=2)
            yd = _full_attn(colc, cc(P_DQ), None, None, qkv_c, colc, P_DK, None, cc(P_DV), n_qp=2, tq=c_len,
                            n_groups=2, sink=sink[l])
            h_c = _merge(h_c, mod, g, (ya, yb, yc, yd), w_gate, w_br, w_o, mod_row=c_row, tm=c_len)
            h_c = _ffn(h_c, mod, g, wg[1], wu[1], wd[1], k=2, mod_row=c_row, tm=c_len)
    return h_x
```

```python
import functools
import math

import jax
import jax.numpy as jnp
from jax import lax
from jax.experimental import pallas as pl
from jax.experimental.pallas import tpu as pltpu

HEAD_DIM = 64
LANES = 128
GRID_W = 64
NA_KH = 8
NA_KW = 16
NA_QROWS = 4
NA_SLAB = NA_QROWS + NA_KH
SWA_WINDOW = 128
SWA_TQ = 256
KV_CHUNK = 1024
ATTN_COL_BLOCK = 512
ATTN_PAIR_UNROLL = 2
FFN_RES_WEIGHT = 0.5
ROPE_THETA = 10000.0
NORM_EPS = 1e-6
N_MOD = 9
NEG = -1e30
VMEM_LIMIT = 48 * 1024 * 1024
BF16 = jnp.bfloat16
F32 = jnp.float32

P_AQ, P_AK, P_AV = 0, 4, 8
P_BQ, P_BK, P_BV = 12, 16, 20
P_CQ, P_CK, P_CV = 24, 28, 30
P_DQ, P_DK, P_DV = 32, 36, 38
N_PAIRS = 40


def _pairs(first, n):
    return tuple(range(first, first + n))


COLS_X = _pairs(P_AQ, 4) + _pairs(P_AV, 4) + _pairs(P_CQ, 4) + _pairs(P_CV, 2) + _pairs(P_BQ, 4) + _pairs(P_DQ, 4)
COLS_C = COLS_X + _pairs(P_BV, 4) + _pairs(P_DV, 2)
COLS128_X = _pairs(P_BV, 4) + _pairs(P_DV, 2)
SEGMENTS = (
    (P_AQ, 4, None, True, True), (P_AK, 4, None, True, False), (P_AV, 4, None, False, False),
    (P_BQ, 4, None, False, True), (P_BK, 4, None, False, False), (P_BV, 4, None, False, False),
    (P_CQ, 4, 0, True, True), (P_CK, 2, 1, True, False), (P_CV, 2, None, False, False),
    (P_DQ, 4, None, True, True), (P_DK, 2, None, True, False), (P_DV, 2, None, False, False),
)
LOG2_E = math.log2(math.e)
Q_SCALE = HEAD_DIM ** -0.5 * LOG2_E


def _const_spec(shape):
    zeros = (0,) * len(shape)
    return pl.BlockSpec(shape, lambda *_: zeros, pipeline_mode=pl.Buffered(1))


def _params(*sem):
    return pltpu.CompilerParams(dimension_semantics=sem, vmem_limit_bytes=VMEM_LIMIT)


def _rms(x):
    return x * lax.rsqrt(jnp.mean(x * x, axis=-1, keepdims=True) + NORM_EPS)


def _sigmoid(x):
    return 1.0 / (1.0 + jnp.exp(-x))


def _dot(a, b):
    return jnp.dot(a, b, preferred_element_type=F32)


def _mod_kernel(ct_ref, w_ref, b_ref, o_ref, *, n_rows):
    ct = ct_ref[...]
    s = ct * _sigmoid(ct)
    w = w_ref[...]
    for r in range(n_rows):
        o_ref[r:r + 1, :] = jnp.sum(w * s[:, r:r + 1], axis=0, keepdims=True) + b_ref[...]


def _modulation(cond_t, w_mod, b_mod, n_rows):
    d, n = w_mod.shape
    tn = 1024
    return pl.pallas_call(
        functools.partial(_mod_kernel, n_rows=n_rows),
        grid=(n // tn,),
        in_specs=[pl.BlockSpec((d, 8), lambda j: (0, 0)),
                  pl.BlockSpec((d, tn), lambda j: (0, j)),
                  pl.BlockSpec((1, tn), lambda j: (0, j))],
        out_specs=pl.BlockSpec((n_rows, tn), lambda j: (0, j)),
        out_shape=jax.ShapeDtypeStruct((n_rows, n), F32),
        compiler_params=_params("parallel"),
    )(cond_t, w_mod, b_mod.reshape(1, n))


def _ffn_kernel(h_ref, mod_ref, g_ref, wg_ref, wu_ref, wd_ref, o_ref, *, k, n_chunks):
    h = h_ref[...]
    mod = mod_ref[...]
    g = g_ref[...]
    u = (_rms(h) * g[2 * k:2 * k + 1] * (1.0 + mod[3 * k + 1:3 * k + 2]) + mod[3 * k:3 * k + 1]).astype(BF16)
    cf = wg_ref.shape[1] // n_chunks
    y = None
    for c in range(n_chunks):
        gate = _dot(u, wg_ref[:, c * cf:(c + 1) * cf])
        up = _dot(u, wu_ref[:, c * cf:(c + 1) * cf])
        act = (gate * _sigmoid(gate) * up).astype(BF16)
        part = _dot(act, wd_ref[c * cf:(c + 1) * cf, :])
        y = part if y is None else y + part
    o_ref[...] = h + FFN_RES_WEIGHT * mod[3 * k + 2:3 * k + 3] * (_rms(y) * g[2 * k + 1:2 * k + 2])


def _ffn(h, mod, gains, wg, wu, wd, *, k, mod_row, tm):
    bsz, t, d = h.shape
    f = wg.shape[1]
    return pl.pallas_call(
        functools.partial(_ffn_kernel, k=k, n_chunks=2),
        grid=(bsz, t // tm),
        in_specs=[pl.BlockSpec((None, tm, d), lambda b, i: (b, i, 0)),
                  pl.BlockSpec((None, N_MOD, d), lambda b, i: (mod_row(b), 0, 0)),
                  _const_spec((6, d)), _const_spec((d, f)), _const_spec((d, f)), _const_spec((f, d))],
        out_specs=pl.BlockSpec((None, tm, d), lambda b, i: (b, i, 0)),
        out_shape=jax.ShapeDtypeStruct((bsz, t, d), F32),
        compiler_params=_params("parallel", "parallel"),
    )(h, mod, gains, wg, wu, wd)


def _inproj_kernel(*refs, rope, cols, cols128):
    refs = list(refs)
    h_ref, mod_ref, g_ref, w_ref, qkg_ref, gm_ref = refs[:6]
    cos_ref, sa_ref, sb_ref = refs[6:9] if rope else (None, None, None)
    o_ref, t_ref = refs[9:11] if rope else refs[6:8]
    t128_ref = refs[-1] if cols128 else None
    h = h_ref[...]
    mod = mod_ref[...]
    u = (_rms(h) * g_ref[2:3, :] * (1.0 + mod[4:5]) + mod[3:4]).astype(BF16)
    for p0, n_p, norm_row, rotary, scaled in SEGMENTS:
        r = _dot(u, w_ref[:, p0 * LANES:(p0 + n_p) * LANES])
        for j in range(n_p):
            x = r[:, j * LANES:(j + 1) * LANES]
            if norm_row is not None:
                xx = x * x
                hi = xx.astype(BF16)
                lo = (xx - hi.astype(F32)).astype(BF16)
                ms = _dot(hi, gm_ref[...]) + _dot(lo, gm_ref[...])
                x = x * lax.rsqrt(ms + NORM_EPS) * qkg_ref[norm_row:norm_row + 1, :]
            if rotary and rope:
                x = (x * cos_ref[...] + pltpu.roll(x, LANES - 16, 1) * sa_ref[...]
                     + pltpu.roll(x, 16, 1) * sb_ref[...])
            if scaled:
                x = x * Q_SCALE
            o_ref[p0 + j] = x.astype(BF16)
            if p0 + j in cols:
                t_ref[cols.index(p0 + j)] = x.T.astype(BF16)
            if p0 + j in cols128:
                for t in range(x.shape[0] // LANES):
                    t128_ref[cols128.index(p0 + j), t] = x[t * LANES:(t + 1) * LANES, :].T.astype(BF16)


def _inproj(h, mod, gains, w_qkv, qk_gain, gmat, rope_tabs, *, mod_row, tm, cols, cols128=()):
    bsz, t, d = h.shape
    rope = rope_tabs is not None
    chunk = min(KV_CHUNK, t)
    per_chunk = chunk // tm
    in_specs = [pl.BlockSpec((None, tm, d), lambda b, i: (b, i, 0)),
                pl.BlockSpec((None, N_MOD, d), lambda b, i: (mod_row(b), 0, 0)),
                _const_spec((6, d)), _const_spec(w_qkv.shape), _const_spec((2, LANES)),
                _const_spec((LANES, LANES))]
    args = [h, mod, gains, w_qkv, qk_gain, gmat]
    if rope:
        in_specs += [pl.BlockSpec((tm, LANES), lambda b, i: (i, 0))] * 3
        args += list(rope_tabs)
    out_specs = [pl.BlockSpec((None, N_PAIRS, tm, LANES), lambda b, i: (b, 0, i, 0)),
                 pl.BlockSpec((None, len(cols), None, LANES, tm),
                              lambda b, i: (b, 0, i // per_chunk, 0, i % per_chunk))]
    out_shape = [jax.ShapeDtypeStruct((bsz, N_PAIRS, t, LANES), BF16),
                 jax.ShapeDtypeStruct((bsz, len(cols), t // chunk, LANES, chunk), BF16)]
    if cols128:
        out_specs.append(pl.BlockSpec((None, len(cols128), tm // LANES, LANES, LANES), lambda b, i: (b, 0, i, 0, 0)))
        out_shape.append(jax.ShapeDtypeStruct((bsz, len(cols128), t // LANES, LANES, LANES), BF16))
    return pl.pallas_call(
        functools.partial(_inproj_kernel, rope=rope, cols=cols, cols128=cols128),
        grid=(bsz, t // tm),
        in_specs=in_specs,
        out_specs=out_specs,
        out_shape=out_shape,
        compiler_params=_params("parallel", "parallel"),
    )(*args)


def _top_half(shape):
    return lax.broadcasted_iota(jnp.int32, shape, 0) < HEAD_DIM


def _stack_heads_t(qt_ref, n_qp):
    parts = []
    for p in range(n_qp):
        qt = qt_ref[p]
        top = _top_half(qt.shape)
        parts += [jnp.where(top, qt, jnp.zeros_like(qt)), jnp.where(top, jnp.zeros_like(qt), qt)]
    return jnp.concatenate(parts, axis=1)


def _merge_heads_t(o_t, p, tq):
    even = o_t[:, (2 * p) * tq:(2 * p + 1) * tq]
    odd = o_t[:, (2 * p + 1) * tq:(2 * p + 2) * tq]
    return jnp.where(_top_half(even.shape), even, odd).T


def _sink_row(sink_ref, kv_head, n_heads, tq):
    col = lax.broadcasted_iota(jnp.int32, (1, n_heads * tq), 1)
    row = jnp.full((1, n_heads * tq), sink_ref[kv_head * n_heads], F32)
    for gq in range(1, n_heads):
        row = jnp.where(col >= gq * tq, sink_ref[kv_head * n_heads + gq], row)
    return row * LOG2_E


def _value_slab_t(vt_ref, start, n_keys):
    c0 = start // LANES
    return jnp.concatenate([vt_ref[c0 + t] for t in range(n_keys // LANES)], axis=1)


def _full_attn_kernel(*refs, n_qp, use_x, use_sink, diff, col_blk, lam_init):
    refs = list(refs)
    qt_ref = refs.pop(0)
    kx_ref, vxt_ref = (refs.pop(0), refs.pop(0)) if use_x else (None, None)
    kc_ref, vct_ref = refs.pop(0), refs.pop(0)
    sink_ref = refs.pop(0) if use_sink else None
    dl_ref, sub_ref = (refs.pop(0), refs.pop(0)) if diff else (None, None)
    o_ref, m_ref, l_ref, acc_ref = refs[:4]
    sa_ref, sb_ref, xa_ref, xb_ref = refs[4:] if use_x else (None,) * 4
    tq = qt_ref.shape[2]
    n_heads = 2 * n_qp
    qs_t = _stack_heads_t(qt_ref, n_qp)

    if use_sink:
        m_ref[...] = _sink_row(sink_ref, pl.program_id(1), n_heads, tq)
        l_ref[...] = jnp.ones(l_ref.shape, F32)
    else:
        m_ref[...] = jnp.full(m_ref.shape, NEG, F32)
        l_ref[...] = jnp.zeros(l_ref.shape, F32)
    acc_ref[...] = jnp.zeros(acc_ref.shape, F32)

    n_blk = qs_t.shape[1] // col_blk

    def blk(c):
        return slice(c * col_blk, (c + 1) * col_blk)

    def scores(k, c):
        return _dot(k, qs_t[:, blk(c)])

    def softmax_pv(s, s_max, vt, c):
        m_prev = m_ref[:, blk(c)]
        m_new = jnp.maximum(m_prev, s_max)
        alpha = jnp.exp2(m_prev - m_new)
        p = jnp.exp2(s - m_new)
        l_ref[:, blk(c)] = alpha * l_ref[:, blk(c)] + jnp.sum(p, axis=0, keepdims=True)
        m_ref[:, blk(c)] = m_new
        acc_ref[:, blk(c)] = alpha * acc_ref[:, blk(c)] + _dot(vt, p.astype(BF16))

    def ctx_chunk(c):
        s = scores(kc_ref[...], c)
        softmax_pv(s, jnp.max(s, axis=0, keepdims=True), vct_ref[...], c)

    if not use_x:
        for c in range(n_blk):
            ctx_chunk(c)
    else:
        n_chunk, _, tk = vxt_ref.shape
        buf_a, buf_b = (sa_ref, xa_ref), (sb_ref, xb_ref)

        def k_chunk(j):
            return kx_ref[pl.ds(pl.multiple_of(j * tk, tk), tk), :]

        def scores_into(buf, k, c):
            s = scores(k, c)
            buf[0][:, blk(c)] = s
            buf[1][:, blk(c)] = jnp.max(s, axis=0, keepdims=True)

        def half_step(cur, nxt, j):
            k_next = k_chunk(jnp.minimum(j + 1, n_chunk - 1))
            vt = vxt_ref[j]
            for c in range(n_blk):
                scores_into(nxt, k_next, c)
                softmax_pv(cur[0][:, blk(c)], cur[1][:, blk(c)], vt, c)

        k0 = k_chunk(0)
        for c in range(n_blk):
            scores_into(buf_a, k0, c)
            ctx_chunk(c)

        unroll = 2 * ATTN_PAIR_UNROLL

        def body(jj, carry):
            for u in range(0, unroll, 2):
                half_step(buf_a, buf_b, unroll * jj + u)
                half_step(buf_b, buf_a, unroll * jj + u + 1)
            return carry
        n_body = n_chunk // unroll
        lax.fori_loop(0, n_body, body, 0)
        for j in range(n_body * unroll, n_chunk):
            half_step(*((buf_a, buf_b) if j % 2 == 0 else (buf_b, buf_a)), j)

    o_t = acc_ref[...] / l_ref[...]
    if diff:
        dl = dl_ref[...]
        lam = (jnp.exp(jnp.sum(dl[0:1] * dl[1:2], axis=-1, keepdims=True))
               - jnp.exp(jnp.sum(dl[2:3] * dl[3:4], axis=-1, keepdims=True)) + lam_init)
        d = (o_t[:, :tq] - lam * o_t[:, tq:]).T
        o_ref[0] = (_rms(d) * sub_ref[...] * (1.0 - lam_init)).astype(o_ref.dtype)
    else:
        for p in range(n_qp):
            o_ref[p] = _merge_heads_t(o_t, p, tq).astype(o_ref.dtype)


def _full_attn(q_cols, q_col0, rows_x, cols_x, rows_c, cols_c, k_pair0, vx_col0, vc_col0, *, n_qp, tq,
               n_groups, sink=None, diff_lambda=None, diff_subln=None, lam_init=0.0):
    bsz, _, n_qchunk, _, q_chunk = q_cols.shape
    t_q = n_qchunk * q_chunk
    per_chunk = q_chunk // tq
    c_len = rows_c.shape[2]
    use_x = rows_x is not None
    diff = diff_lambda is not None
    m_cols = 2 * n_qp * tq
    qp_blk = q_col0 // n_qp
    in_specs = [pl.BlockSpec((None, n_qp, None, LANES, tq),
                             lambda b, g, i: (b, qp_blk + g, i // per_chunk, 0, i % per_chunk))]
    args = [q_cols]
    if use_x:
        s_len = rows_x.shape[2]
        n_chunk, tk = cols_x.shape[2], cols_x.shape[4]
        in_specs += [pl.BlockSpec((None, None, s_len, LANES), lambda b, g, i: (b, k_pair0 + g, 0, 0)),
                     pl.BlockSpec((None, None, n_chunk, LANES, tk), lambda b, g, i: (b, vx_col0 + g, 0, 0, 0))]
        args += [rows_x, cols_x]
    in_specs += [pl.BlockSpec((None, None, c_len, LANES), lambda b, g, i: (b, k_pair0 + g, 0, 0)),
                 pl.BlockSpec((None, None, None, LANES, c_len), lambda b, g, i: (b, vc_col0 + g, 0, 0, 0))]
    args += [rows_c, cols_c]
    if sink is not None:
        in_specs.append(pl.BlockSpec(memory_space=pltpu.SMEM))
        args.append(sink)
    if diff:
        in_specs += [_const_spec((4, HEAD_DIM)), _const_spec((1, LANES))]
        args += [diff_lambda, diff_subln.reshape(1, LANES)]
    return pl.pallas_call(
        functools.partial(_full_attn_kernel, n_qp=n_qp, use_x=use_x, use_sink=sink is not None, diff=diff,
                          col_blk=min(ATTN_COL_BLOCK, m_cols), lam_init=lam_init),
        grid=(bsz, n_groups, t_q // tq),
        in_specs=in_specs,
        out_specs=pl.BlockSpec((None, n_qp, tq, LANES), lambda b, g, i: (b, g, i, 0)),
        out_shape=jax.ShapeDtypeStruct((bsz, n_groups * n_qp, t_q, LANES), BF16),
        scratch_shapes=[pltpu.VMEM((1, m_cols), F32), pltpu.VMEM((1, m_cols), F32),
                        pltpu.VMEM((LANES, m_cols), F32)]
        + ([pltpu.VMEM((cols_x.shape[4], m_cols), F32)] * 2 + [pltpu.VMEM((1, m_cols), F32)] * 2
           if use_x else []),
        compiler_params=_params("parallel", "parallel", "parallel"),
    )(*args)


def _na_kernel(qt_ref, kx_ref, vt_ref, kc_ref, vct_ref, bias_ref, o_ref, *, grid_rows):
    j = pl.program_id(2)
    tq = qt_ref.shape[1]
    n_keys = NA_SLAB * GRID_W
    row0 = jnp.clip(NA_QROWS * j - NA_KH // 2, 0, grid_rows - NA_SLAB)
    start = pl.multiple_of(row0 * GRID_W, NA_QROWS * GRID_W)
    k_slab = kx_ref[pl.ds(start, n_keys), :]
    vt_slab = _value_slab_t(vt_ref, start, n_keys)
    qt = qt_ref[...]
    top = _top_half(qt.shape)
    qs_t = jnp.concatenate([jnp.where(top, qt, jnp.zeros_like(qt)), jnp.where(top, jnp.zeros_like(qt), qt)], axis=1)
    s_loc = _dot(k_slab, qs_t) + jnp.concatenate([bias_ref[0], bias_ref[1]], axis=1)
    s_ctx = _dot(kc_ref[...], qs_t)
    m = jnp.maximum(jnp.max(s_loc, axis=0, keepdims=True), jnp.max(s_ctx, axis=0, keepdims=True))
    p_loc = jnp.exp2(s_loc - m)
    p_ctx = jnp.exp2(s_ctx - m)
    l = jnp.sum(p_loc, axis=0, keepdims=True) + jnp.sum(p_ctx, axis=0, keepdims=True)
    o_t = (_dot(vt_slab, p_loc.astype(BF16)) + _dot(vct_ref[...], p_ctx.astype(BF16))) / l
    o_ref[...] = _merge_heads_t(o_t, 0, tq).astype(o_ref.dtype)


def _na_bias_tables(na_bias, grid_rows):
    n_heads, n_dr, n_dc = na_bias.shape
    n_blk = grid_rows // NA_QROWS
    plan = []
    for j in (0, 1, n_blk - 1):
        row0 = min(max(NA_QROWS * j - NA_KH // 2, 0), grid_rows - NA_SLAB)
        rows = []
        for i in range(NA_QROWS):
            r = NA_QROWS * j + i
            r0 = min(max(r - NA_KH // 2, 0), grid_rows - NA_KH)
            rows.append(tuple((row0 + t - r + NA_KH - 1) if r0 <= row0 + t < r0 + NA_KH else None
                              for t in range(NA_SLAB)))
        plan.append(tuple(rows))
    padded = jnp.pad(jnp.flip(na_bias, axis=2), ((0, 0), (0, 16 - n_dr), (0, LANES - n_dc)))
    return pl.pallas_call(
        functools.partial(_na_bias_kernel, plan=tuple(plan)),
        grid=(n_heads,),
        in_specs=[pl.BlockSpec((None, 16, LANES), lambda h: (h, 0, 0))],
        out_specs=pl.BlockSpec((3, None, NA_SLAB * GRID_W, NA_QROWS * GRID_W), lambda h: (0, h, 0, 0)),
        out_shape=jax.ShapeDtypeStruct((3, n_heads, NA_SLAB * GRID_W, NA_QROWS * GRID_W), F32),
        compiler_params=_params("parallel"),
    )(padded)


def _na_bias_kernel(b_ref, o_ref, *, plan):
    shape = (GRID_W, LANES)
    lane = lax.broadcasted_iota(jnp.int32, shape, 1)
    kc = lax.broadcasted_iota(jnp.int32, shape, 0)
    cq = lane % GRID_W
    c0 = jnp.clip(cq - NA_KW // 2, 0, GRID_W - NA_KW)
    col_bias = jnp.where(kc >= c0, jnp.where(kc < c0 + NA_KW, 0.0, NEG), NEG).astype(F32)
    left_half = lane < GRID_W
    neg = jnp.full(shape, NEG, F32)
    used = sorted({dr for cfg in plan for row in cfg for dr in row if dr is not None})
    shift = LANES - (NA_KW - 1)
    tile_lo, tile_hi = {}, {}
    for dr in used:
        row = jnp.broadcast_to(b_ref[dr:dr + 1, :] * LOG2_E, shape)
        tile_lo[dr] = pltpu.roll(row, shift, 1, stride=1, stride_axis=0)
        tile_hi[dr] = pltpu.roll(row, (shift + GRID_W) % LANES, 1, stride=1, stride_axis=0)
    for cfg, cfg_rows in enumerate(plan):
        for t in range(NA_SLAB):
            for u in range(NA_QROWS // 2):
                dr_left, dr_right = cfg_rows[2 * u][t], cfg_rows[2 * u + 1][t]
                left = neg if dr_left is None else tile_lo[dr_left]
                right = neg if dr_right is None else tile_hi[dr_right]
                blk = jnp.where(col_bias < 0.0, neg, jnp.where(left_half, left, right))
                o_ref[cfg, t * GRID_W:(t + 1) * GRID_W, u * LANES:(u + 1) * LANES] = blk


def _na_attn(qkv_x, colx, v128_x, qkv_c, colc, bias_tabs):
    bsz, _, s_len, _ = qkv_x.shape
    c_len = qkv_c.shape[2]
    grid_rows = s_len // GRID_W
    n_blk = grid_rows // NA_QROWS
    tq = NA_QROWS * GRID_W
    n_keys = NA_SLAB * GRID_W
    per_chunk = colx.shape[4] // tq
    q0, v0, vc0 = COLS_X.index(P_BQ), COLS128_X.index(P_BV), COLS_C.index(P_BV)

    def cfg(j):
        return jnp.where(j == 0, 0, jnp.where(j == n_blk - 1, 2, 1))

    return pl.pallas_call(
        functools.partial(_na_kernel, grid_rows=grid_rows),
        grid=(bsz, 4, n_blk),
        in_specs=[pl.BlockSpec((None, None, None, LANES, tq),
                               lambda b, p, j: (b, q0 + p, j // per_chunk, 0, j % per_chunk)),
                  pl.BlockSpec((None, None, s_len, LANES), lambda b, p, j: (b, P_BK + p, 0, 0)),
                  pl.BlockSpec((None, None, s_len // LANES, LANES, LANES), lambda b, p, j: (b, v0 + p, 0, 0, 0)),
                  pl.BlockSpec((None, None, c_len, LANES), lambda b, p, j: (b, P_BK + p, 0, 0)),
                  pl.BlockSpec((None, None, None, LANES, c_len), lambda b, p, j: (b, vc0 + p, 0, 0, 0)),
                  pl.BlockSpec((None, 2, n_keys, tq), lambda b, p, j: (cfg(j), p, 0, 0))],
        out_specs=pl.BlockSpec((None, None, tq, LANES), lambda b, p, j: (b, p, j, 0)),
        out_shape=jax.ShapeDtypeStruct((bsz, 4, s_len, LANES), BF16),
        compiler_params=_params("parallel", "parallel", "parallel"),
    )(colx, qkv_x, v128_x, qkv_c, colc, bias_tabs)


def _swa_kernel(qt_ref, kx_ref, vt_ref, kc_ref, vct_ref, sink_ref, o_ref):
    n_qp, _, tq = qt_ref.shape
    n_heads = 2 * n_qp
    s_len = kx_ref.shape[0]
    n_keys = tq + 2 * SWA_WINDOW
    q0 = pl.program_id(2) * tq
    start = pl.multiple_of(jnp.clip(q0 - SWA_WINDOW, 0, s_len - n_keys), SWA_WINDOW)
    k_slab = kx_ref[pl.ds(start, n_keys), :]
    vt_slab = _value_slab_t(vt_ref, start, n_keys)
    qs_t = _stack_heads_t(qt_ref, n_qp)
    kpos = start + lax.broadcasted_iota(jnp.int32, (n_keys, tq), 0)
    qpos = q0 + lax.broadcasted_iota(jnp.int32, (n_keys, tq), 1)
    band = jnp.where(jnp.abs(kpos - qpos) <= SWA_WINDOW, 0.0, NEG).astype(F32)
    s_loc = _dot(k_slab, qs_t) + jnp.concatenate([band] * n_heads, axis=1)
    s_ctx = _dot(kc_ref[...], qs_t)
    sink = _sink_row(sink_ref, pl.program_id(1), n_heads, tq)
    m = jnp.maximum(jnp.maximum(jnp.max(s_loc, axis=0, keepdims=True), jnp.max(s_ctx, axis=0, keepdims=True)),
                    sink)
    p_loc = jnp.exp2(s_loc - m)
    p_ctx = jnp.exp2(s_ctx - m)
    l = jnp.sum(p_loc, axis=0, keepdims=True) + jnp.sum(p_ctx, axis=0, keepdims=True) + jnp.exp2(sink - m)
    o_t = (_dot(vt_slab, p_loc.astype(BF16)) + _dot(vct_ref[...], p_ctx.astype(BF16))) / l
    for p in range(n_qp):
        o_ref[p] = _merge_heads_t(o_t, p, tq).astype(o_ref.dtype)


def _swa_attn(qkv_x, colx, v128_x, qkv_c, colc, sink):
    bsz, _, s_len, _ = qkv_x.shape
    c_len = qkv_c.shape[2]
    n_qp = 2
    tq = min(SWA_TQ, s_len)
    per_chunk = colx.shape[4] // tq
    q0, v0, vc0 = COLS_X.index(P_DQ) // n_qp, COLS128_X.index(P_DV), COLS_C.index(P_DV)
    return pl.pallas_call(
        _swa_kernel,
        grid=(bsz, 2, s_len // tq),
        in_specs=[pl.BlockSpec((None, n_qp, None, LANES, tq),
                               lambda b, g, i: (b, q0 + g, i // per_chunk, 0, i % per_chunk)),
                  pl.BlockSpec((None, None, s_len, LANES), lambda b, g, i: (b, P_DK + g, 0, 0)),
                  pl.BlockSpec((None, None, s_len // LANES, LANES, LANES), lambda b, g, i: (b, v0 + g, 0, 0, 0)),
                  pl.BlockSpec((None, None, c_len, LANES), lambda b, g, i: (b, P_DK + g, 0, 0)),
                  pl.BlockSpec((None, None, None, LANES, c_len), lambda b, g, i: (b, vc0 + g, 0, 0, 0)),
                  pl.BlockSpec(memory_space=pltpu.SMEM)],
        out_specs=pl.BlockSpec((None, n_qp, tq, LANES), lambda b, g, i: (b, g, i, 0)),
        out_shape=jax.ShapeDtypeStruct((bsz, 2 * n_qp, s_len, LANES), BF16),
        compiler_params=_params("parallel", "parallel", "parallel"),
    )(colx, qkv_x, v128_x, qkv_c, colc, sink)


def _merge_kernel(h_ref, mod_ref, g_ref, ya_ref, yb_ref, yc_ref, yd_ref, wgate_ref, wbr_ref, wout_ref, o_ref):
    h = h_ref[...]
    mod = mod_ref[...]
    g = g_ref[...]
    d = h.shape[1]
    u = (_rms(h) * g[2:3] * (1.0 + mod[4:5]) + mod[3:4]).astype(BF16)
    acc = None
    for i, y_ref in enumerate((ya_ref, yb_ref, yc_ref, yd_ref)):
        y = jnp.concatenate([y_ref[p] for p in range(y_ref.shape[0])], axis=-1)
        term = _sigmoid(_dot(u, wgate_ref[:, i * d:(i + 1) * d])) * _dot(y, wbr_ref[i])
        acc = term if acc is None else acc + term
    out = _dot(acc.astype(BF16), wout_ref[...])
    o_ref[...] = h + mod[5:6] * (_rms(out) * g[3:4])


def _merge(h, mod, gains, ys, w_gate, w_branch, w_out, *, mod_row, tm):
    bsz, t, d = h.shape
    n_yp = ys[0].shape[1]
    y_spec = pl.BlockSpec((None, n_yp, tm, LANES), lambda b, i: (b, 0, i, 0))
    return pl.pallas_call(
        _merge_kernel,
        grid=(bsz, t // tm),
        in_specs=[pl.BlockSpec((None, tm, d), lambda b, i: (b, i, 0)),
                  pl.BlockSpec((None, N_MOD, d), lambda b, i: (mod_row(b), 0, 0)),
                  _const_spec((6, d)), y_spec, y_spec, y_spec, y_spec,
                  _const_spec(w_gate.shape), _const_spec(w_branch.shape), _const_spec(w_out.shape)],
        out_specs=pl.BlockSpec((None, tm, d), lambda b, i: (b, i, 0)),
        out_shape=jax.ShapeDtypeStruct((bsz, t, d), F32),
        compiler_params=_params("parallel", "parallel"),
    )(h, mod, gains, *ys, w_gate, w_branch, w_out)


def _rope_tables(n_tok):
    t = jnp.arange(n_tok, dtype=jnp.int32)
    row = (t // GRID_W).astype(F32)
    col = (t % GRID_W).astype(F32)
    n_freq = HEAD_DIM // 4
    inv_freq = ROPE_THETA ** (-jnp.arange(n_freq, dtype=F32) / n_freq)
    lane = jnp.arange(LANES, dtype=jnp.int32)
    axis = (lane % HEAD_DIM) // (HEAD_DIM // 2)
    freq = inv_freq[lane % n_freq]
    ang = jnp.where(axis[None, :] == 0, row[:, None], col[:, None]) * freq[None, :]
    first = (lane % (HEAD_DIM // 2)) < n_freq
    cos, sin = jnp.cos(ang), jnp.sin(ang)
    return cos, jnp.where(first[None, :], -sin, 0.0), jnp.where(first[None, :], 0.0, sin)


def _dup_heads(w):
    h0, h1 = w[:, :HEAD_DIM], w[:, HEAD_DIM:]
    return jnp.concatenate([h0, h0, h1, h1], axis=1)


def _qkv_weight(w_in):
    o = 0
    cols = []
    for width in (1536, 1536):
        cols.append(w_in[:, o:o + width])
        o += width
    for _ in range(2):
        cols += [w_in[:, o:o + 512], _dup_heads(w_in[:, o + 512:o + 640]), _dup_heads(w_in[:, o + 640:o + 768])]
        o += 768
    return jnp.concatenate(cols, axis=1).astype(BF16), o


def kernel(x, c, ctx, c_ctx, w_mod, b_mod, norm_gain, ffn_w_gate, ffn_w_up, ffn_w_down, w_in, w_branch,
           w_out, diff_lambda, diff_subln, na_bias, qk_norm, sink):
    bsz, s_len, d = x.shape
    c_len = ctx.shape[1]
    depth = w_mod.shape[0]
    tm = min(512, s_len)
    cond_t = jnp.zeros((d, 8), F32).at[:, :bsz].set(c.T).at[:, bsz].set(c_ctx)
    rope_tabs = _rope_tables(s_len)
    grp = jnp.arange(LANES, dtype=jnp.int32) // HEAD_DIM
    gmat = jnp.where(grp[:, None] == grp[None, :], 1.0 / HEAD_DIM, 0.0).astype(BF16)
    x_row = lambda b: b
    c_row = lambda b: bsz
    cx, cc = COLS_X.index, COLS_C.index

    h_x, h_c = x, ctx
    for l in range(depth):
        last = l == depth - 1
        lam_init = 0.8 - 0.6 * math.exp(-0.3 * l)
        mod = _modulation(cond_t, w_mod[l], b_mod[l], bsz + 1).reshape(bsz + 1, N_MOD, d)
        g = norm_gain[l]
        wg, wu, wd = ffn_w_gate[l].astype(BF16), ffn_w_up[l].astype(BF16), ffn_w_down[l].astype(BF16)
        w_qkv, gate_off = _qkv_weight(w_in[l])
        w_gate = w_in[l][:, gate_off:].astype(BF16)
        w_br = w_branch[l].astype(BF16)
        w_o = w_out[l].astype(BF16)
        qk_gain = jnp.tile(qk_norm[l], (1, 2))

        h_x = _ffn(h_x, mod, g, wg[0], wu[0], wd[0], k=0, mod_row=x_row, tm=tm)
        h_c = _ffn(h_c, mod, g, wg[0], wu[0], wd[0], k=0, mod_row=c_row, tm=c_len)

        qkv_x, colx, v128_x = _inproj(h_x, mod, g, w_qkv, qk_gain, gmat, rope_tabs, mod_row=x_row, tm=tm,
                                      cols=COLS_X, cols128=COLS128_X)
        qkv_c, colc = _inproj(h_c, mod, g, w_qkv, qk_gain, gmat, None, mod_row=c_row, tm=c_len, cols=COLS_C)

        ya = _full_attn(colx, cx(P_AQ), qkv_x, colx, qkv_c, colc, P_AK, cx(P_AV), cc(P_AV), n_qp=1,
                        tq=min(512, s_len), n_groups=4,
                        diff_lambda=diff_lambda[l], diff_subln=diff_subln[l], lam_init=lam_init)
        yb = _na_attn(qkv_x, colx, v128_x, qkv_c, colc, _na_bias_tables(na_bias[l], s_len // GRID_W))
        yc = _full_attn(colx, cx(P_CQ), qkv_x, colx, qkv_c, colc, P_CK, cx(P_CV), cc(P_CV), n_qp=2,
                        tq=min(256, s_len), n_groups=2)
        yd = _swa_attn(qkv_x, colx, v128_x, qkv_c, colc, sink[l])
        h_x = _merge(h_x, mod, g, (ya, yb, yc, yd), w_gate, w_br, w_o, mod_row=x_row, tm=tm)
        h_x = _ffn(h_x, mod, g, wg[1], wu[1], wd[1], k=2, mod_row=x_row, tm=tm)

        if not last:
            ya = _full_attn(colc, cc(P_AQ), None, None, qkv_c, colc, P_AK, None, cc(P_AV), n_qp=1, tq=c_len,
                            n_groups=4, diff_lambda=diff_lambda[l], diff_subln=diff_subln[l], lam_init=lam_init)
            yb = _full_attn(colc, cc(P_BQ), None, None, qkv_c, colc, P_BK, None, cc(P_BV), n_qp=1, tq=c_len,
                            n_groups=4)
            yc = _full_attn(colc, cc(P_CQ), None, None, qkv_c, colc, P_CK, None, cc(P_CV), n_qp=2, tq=c_len,
                            n_groups=2)
            yd = _full_attn(colc, cc(P_DQ), None, None, qkv_c, colc, P_DK, None, cc(P_DV), n_qp=2, tq=c_len,
                            n_groups=2, sink=sink[l])
            h_c = _merge(h_c, mod, g, (ya, yb, yc, yd), w_gate, w_br, w_o, mod_row=c_row, tm=c_len)
            h_c = _ffn(h_c, mod, g, wg[1], wu[1], wd[1], k=2, mod_row=c_row, tm=c_len)
    return h_x
```

```python
import functools
import math

import jax
import jax.numpy as jnp
from jax import lax
from jax.experimental import pallas as pl
from jax.experimental.pallas import tpu as pltpu

HEAD_DIM = 64
LANES = 128
GRID_W = 64
NA_KH = 8
NA_KW = 16
NA_QROWS = 4
NA_SLAB = NA_QROWS + NA_KH
SWA_WINDOW = 128
SWA_TQ = 256
KV_CHUNK = 1024
ATTN_COL_BLOCK = 512
ATTN_PAIR_UNROLL = 1
FFN_RES_WEIGHT = 0.5
ROPE_THETA = 10000.0
NORM_EPS = 1e-6
N_MOD = 9
NEG = -1e30
VMEM_LIMIT = 48 * 1024 * 1024
BF16 = jnp.bfloat16
F32 = jnp.float32

P_AQ, P_AK, P_AV = 0, 4, 8
P_BQ, P_BK, P_BV = 12, 16, 20
P_CQ, P_CK, P_CV = 24, 28, 30
P_DQ, P_DK, P_DV = 32, 36, 38
N_PAIRS = 40


def _pairs(first, n):
    return tuple(range(first, first + n))


COLS_X = _pairs(P_AQ, 4) + _pairs(P_AV, 4) + _pairs(P_CQ, 4) + _pairs(P_CV, 2) + _pairs(P_BQ, 4) + _pairs(P_DQ, 4)
COLS_C = COLS_X + _pairs(P_BV, 4) + _pairs(P_DV, 2)
COLS128_X = _pairs(P_BV, 4) + _pairs(P_DV, 2)
SEGMENTS = (
    (P_AQ, 4, None, True, True), (P_AK, 4, None, True, False), (P_AV, 4, None, False, False),
    (P_BQ, 4, None, False, True), (P_BK, 4, None, False, False), (P_BV, 4, None, False, False),
    (P_CQ, 4, 0, True, True), (P_CK, 2, 1, True, False), (P_CV, 2, None, False, False),
    (P_DQ, 4, None, True, True), (P_DK, 2, None, True, False), (P_DV, 2, None, False, False),
)
LOG2_E = math.log2(math.e)
Q_SCALE = HEAD_DIM ** -0.5 * LOG2_E


def _const_spec(shape):
    zeros = (0,) * len(shape)
    return pl.BlockSpec(shape, lambda *_: zeros, pipeline_mode=pl.Buffered(1))


def _params(*sem):
    return pltpu.CompilerParams(dimension_semantics=sem, vmem_limit_bytes=VMEM_LIMIT)


def _rms(x):
    return x * lax.rsqrt(jnp.mean(x * x, axis=-1, keepdims=True) + NORM_EPS)


def _sigmoid(x):
    return 1.0 / (1.0 + jnp.exp(-x))


def _dot(a, b):
    return jnp.dot(a, b, preferred_element_type=F32)


def _mod_kernel(ct_ref, w_ref, b_ref, o_ref, *, n_rows):
    ct = ct_ref[...]
    s = ct * _sigmoid(ct)
    w = w_ref[...]
    for r in range(n_rows):
        o_ref[r:r + 1, :] = jnp.sum(w * s[:, r:r + 1], axis=0, keepdims=True) + b_ref[...]


def _modulation(cond_t, w_mod, b_mod, n_rows):
    d, n = w_mod.shape
    tn = 1024
    return pl.pallas_call(
        functools.partial(_mod_kernel, n_rows=n_rows),
        grid=(n // tn,),
        in_specs=[pl.BlockSpec((d, 8), lambda j: (0, 0)),
                  pl.BlockSpec((d, tn), lambda j: (0, j)),
                  pl.BlockSpec((1, tn), lambda j: (0, j))],
        out_specs=pl.BlockSpec((n_rows, tn), lambda j: (0, j)),
        out_shape=jax.ShapeDtypeStruct((n_rows, n), F32),
        compiler_params=_params("parallel"),
    )(cond_t, w_mod, b_mod.reshape(1, n))


def _ffn_kernel(h_ref, mod_ref, g_ref, wg_ref, wu_ref, wd_ref, o_ref, *, k, n_chunks):
    h = h_ref[...]
    mod = mod_ref[...]
    g = g_ref[...]
    u = (_rms(h) * g[2 * k:2 * k + 1] * (1.0 + mod[3 * k + 1:3 * k + 2]) + mod[3 * k:3 * k + 1]).astype(BF16)
    cf = wg_ref.shape[1] // n_chunks
    y = None
    for c in range(n_chunks):
        gate = _dot(u, wg_ref[:, c * cf:(c + 1) * cf])
        up = _dot(u, wu_ref[:, c * cf:(c + 1) * cf])
        act = (gate * _sigmoid(gate) * up).astype(BF16)
        part = _dot(act, wd_ref[c * cf:(c + 1) * cf, :])
        y = part if y is None else y + part
    o_ref[...] = h + FFN_RES_WEIGHT * mod[3 * k + 2:3 * k + 3] * (_rms(y) * g[2 * k + 1:2 * k + 2])


def _ffn(h, mod, gains, wg, wu, wd, *, k, mod_row, tm):
    bsz, t, d = h.shape
    f = wg.shape[1]
    return pl.pallas_call(
        functools.partial(_ffn_kernel, k=k, n_chunks=2),
        grid=(bsz, t // tm),
        in_specs=[pl.BlockSpec((None, tm, d), lambda b, i: (b, i, 0)),
                  pl.BlockSpec((None, N_MOD, d), lambda b, i: (mod_row(b), 0, 0)),
                  _const_spec((6, d)), _const_spec((d, f)), _const_spec((d, f)), _const_spec((f, d))],
        out_specs=pl.BlockSpec((None, tm, d), lambda b, i: (b, i, 0)),
        out_shape=jax.ShapeDtypeStruct((bsz, t, d), F32),
        compiler_params=_params("parallel", "parallel"),
    )(h, mod, gains, wg, wu, wd)


def _inproj_kernel(*refs, rope, cols, cols128):
    refs = list(refs)
    h_ref, mod_ref, g_ref, w_ref, qkg_ref, gm_ref = refs[:6]
    cos_ref, sa_ref, sb_ref = refs[6:9] if rope else (None, None, None)
    o_ref, t_ref = refs[9:11] if rope else refs[6:8]
    t128_ref = refs[-1] if cols128 else None
    h = h_ref[...]
    mod = mod_ref[...]
    u = (_rms(h) * g_ref[2:3, :] * (1.0 + mod[4:5]) + mod[3:4]).astype(BF16)
    for p0, n_p, norm_row, rotary, scaled in SEGMENTS:
        r = _dot(u, w_ref[:, p0 * LANES:(p0 + n_p) * LANES])
        for j in range(n_p):
            x = r[:, j * LANES:(j + 1) * LANES]
            if norm_row is not None:
                xx = x * x
                hi = xx.astype(BF16)
                lo = (xx - hi.astype(F32)).astype(BF16)
                ms = _dot(hi, gm_ref[...]) + _dot(lo, gm_ref[...])
                x = x * lax.rsqrt(ms + NORM_EPS) * qkg_ref[norm_row:norm_row + 1, :]
            if rotary and rope:
                x = (x * cos_ref[...] + pltpu.roll(x, LANES - 16, 1) * sa_ref[...]
                     + pltpu.roll(x, 16, 1) * sb_ref[...])
            if scaled:
                x = x * Q_SCALE
            o_ref[p0 + j] = x.astype(BF16)
            if p0 + j in cols:
                t_ref[cols.index(p0 + j)] = x.T.astype(BF16)
            if p0 + j in cols128:
                for t in range(x.shape[0] // LANES):
                    t128_ref[cols128.index(p0 + j), t] = x[t * LANES:(t + 1) * LANES, :].T.astype(BF16)


def _inproj(h, mod, gains, w_qkv, qk_gain, gmat, rope_tabs, *, mod_row, tm, cols, cols128=()):
    bsz, t, d = h.shape
    rope = rope_tabs is not None
    chunk = min(KV_CHUNK, t)
    per_chunk = chunk // tm
    in_specs = [pl.BlockSpec((None, tm, d), lambda b, i: (b, i, 0)),
                pl.BlockSpec((None, N_MOD, d), lambda b, i: (mod_row(b), 0, 0)),
                _const_spec((6, d)), _const_spec(w_qkv.shape), _const_spec((2, LANES)),
                _const_spec((LANES, LANES))]
    args = [h, mod, gains, w_qkv, qk_gain, gmat]
    if rope:
        in_specs += [pl.BlockSpec((tm, LANES), lambda b, i: (i, 0))] * 3
        args += list(rope_tabs)
    out_specs = [pl.BlockSpec((None, N_PAIRS, tm, LANES), lambda b, i: (b, 0, i, 0)),
                 pl.BlockSpec((None, len(cols), None, LANES, tm),
                              lambda b, i: (b, 0, i // per_chunk, 0, i % per_chunk))]
    out_shape = [jax.ShapeDtypeStruct((bsz, N_PAIRS, t, LANES), BF16),
                 jax.ShapeDtypeStruct((bsz, len(cols), t // chunk, LANES, chunk), BF16)]
    if cols128:
        out_specs.append(pl.BlockSpec((None, len(cols128), tm // LANES, LANES, LANES), lambda b, i: (b, 0, i, 0, 0)))
        out_shape.append(jax.ShapeDtypeStruct((bsz, len(cols128), t // LANES, LANES, LANES), BF16))
    return pl.pallas_call(
        functools.partial(_inproj_kernel, rope=rope, cols=cols, cols128=cols128),
        grid=(bsz, t // tm),
        in_specs=in_specs,
        out_specs=out_specs,
        out_shape=out_shape,
        compiler_params=_params("parallel", "parallel"),
    )(*args)


def _top_half(shape):
    return lax.broadcasted_iota(jnp.int32, shape, 0) < HEAD_DIM


def _stack_heads_t(qt_ref, n_qp):
    parts = []
    for p in range(n_qp):
        qt = qt_ref[p]
        top = _top_half(qt.shape)
        parts += [jnp.where(top, qt, jnp.zeros_like(qt)), jnp.where(top, jnp.zeros_like(qt), qt)]
    return jnp.concatenate(parts, axis=1)


def _merge_heads_t(o_t, p, tq):
    even = o_t[:, (2 * p) * tq:(2 * p + 1) * tq]
    odd = o_t[:, (2 * p + 1) * tq:(2 * p + 2) * tq]
    return jnp.where(_top_half(even.shape), even, odd).T


def _sink_row(sink_ref, kv_head, n_heads, tq):
    col = lax.broadcasted_iota(jnp.int32, (1, n_heads * tq), 1)
    row = jnp.full((1, n_heads * tq), sink_ref[kv_head * n_heads], F32)
    for gq in range(1, n_heads):
        row = jnp.where(col >= gq * tq, sink_ref[kv_head * n_heads + gq], row)
    return row * LOG2_E


def _value_slab_t(vt_ref, start, n_keys):
    c0 = start // LANES
    return jnp.concatenate([vt_ref[c0 + t] for t in range(n_keys // LANES)], axis=1)


def _full_attn_kernel(*refs, n_qp, use_x, use_sink, diff, col_blk, lam_init):
    refs = list(refs)
    qt_ref = refs.pop(0)
    kx_ref, vxt_ref = (refs.pop(0), refs.pop(0)) if use_x else (None, None)
    qn_ref, kn_ref = (refs.pop(0), refs.pop(0)) if use_x else (None, None)
    kc_ref, vct_ref = refs.pop(0), refs.pop(0)
    sink_ref = refs.pop(0) if use_sink else None
    dl_ref, sub_ref = (refs.pop(0), refs.pop(0)) if diff else (None, None)
    o_ref, m_ref, l_ref, acc_ref = refs[:4]
    sa_ref, sb_ref, xa_ref, xb_ref = refs[4:] if use_x else (None,) * 4
    tq = qt_ref.shape[2]
    n_heads = 2 * n_qp
    qs_t = _stack_heads_t(qt_ref, n_qp)

    if use_sink:
        m_ref[...] = _sink_row(sink_ref, pl.program_id(1), n_heads, tq)
        l_ref[...] = jnp.ones(l_ref.shape, F32)
    else:
        m_ref[...] = jnp.full(m_ref.shape, NEG, F32)
        l_ref[...] = jnp.zeros(l_ref.shape, F32)
    acc_ref[...] = jnp.zeros(acc_ref.shape, F32)

    n_blk = qs_t.shape[1] // col_blk

    def blk(c):
        return slice(c * col_blk, (c + 1) * col_blk)

    def scores(k, c, qs=qs_t):
        return _dot(k, qs[:, blk(c)])

    def softmax_pv(s, s_max, vt, c):
        m_prev = m_ref[:, blk(c)]
        m_new = jnp.maximum(m_prev, s_max)
        alpha = jnp.exp2(m_prev - m_new)
        p = jnp.exp2(s - m_new)
        l_ref[:, blk(c)] = alpha * l_ref[:, blk(c)] + jnp.sum(p, axis=0, keepdims=True)
        m_ref[:, blk(c)] = m_new
        acc_ref[:, blk(c)] = alpha * acc_ref[:, blk(c)] + _dot(vt, p.astype(BF16))

    def ctx_chunk(c):
        s = scores(kc_ref[...], c)
        softmax_pv(s, jnp.max(s, axis=0, keepdims=True), vct_ref[...], c)

    if not use_x:
        for c in range(n_blk):
            ctx_chunk(c)
    else:
        n_chunk, _, tk = vxt_ref.shape
        buf_a, buf_b = (sa_ref, xa_ref), (sb_ref, xb_ref)

        def buffers(j):
            return (buf_a, buf_b) if j % 2 == 0 else (buf_b, buf_a)

        def k_chunk(j):
            return kx_ref[pl.ds(pl.multiple_of(j * tk, tk), tk), :]

        def scores_into(buf, k, c, qs=qs_t):
            s = scores(k, c, qs)
            buf[0][:, blk(c)] = s
            buf[1][:, blk(c)] = jnp.max(s, axis=0, keepdims=True)

        def half_step(cur, nxt, j, k_next, qs_next=qs_t):
            vt = vxt_ref[j]
            for c in range(n_blk):
                scores_into(nxt, k_next, c, qs_next)
                softmax_pv(cur[0][:, blk(c)], cur[1][:, blk(c)], vt, c)

        first_step = (pl.program_id(0) + pl.program_id(1) + pl.program_id(2)) == 0

        @pl.when(first_step)
        def _():
            k0 = k_chunk(0)
            for c in range(n_blk):
                scores_into(buf_a, k0, c)

        for c in range(n_blk):
            ctx_chunk(c)

        unroll = 2 * ATTN_PAIR_UNROLL
        n_loop = (n_chunk - 1) // unroll * unroll

        def body(jj, carry):
            for u in range(unroll):
                j = unroll * jj + u
                half_step(*buffers(u), j, k_chunk(j + 1))
            return carry
        lax.fori_loop(0, n_loop // unroll, body, 0)
        for j in range(n_loop, n_chunk - 1):
            half_step(*buffers(j), j, k_chunk(j + 1))
        half_step(*buffers(n_chunk - 1), n_chunk - 1, kn_ref[...], _stack_heads_t(qn_ref, n_qp))
        if n_chunk % 2:
            sa_ref[...] = sb_ref[...]
            xa_ref[...] = xb_ref[...]

    o_t = acc_ref[...] / l_ref[...]
    if diff:
        dl = dl_ref[...]
        lam = (jnp.exp(jnp.sum(dl[0:1] * dl[1:2], axis=-1, keepdims=True))
               - jnp.exp(jnp.sum(dl[2:3] * dl[3:4], axis=-1, keepdims=True)) + lam_init)
        d = (o_t[:, :tq] - lam * o_t[:, tq:]).T
        o_ref[0] = (_rms(d) * sub_ref[...] * (1.0 - lam_init)).astype(o_ref.dtype)
    else:
        for p in range(n_qp):
            o_ref[p] = _merge_heads_t(o_t, p, tq).astype(o_ref.dtype)


def _full_attn(q_cols, q_col0, rows_x, cols_x, rows_c, cols_c, k_pair0, vx_col0, vc_col0, *, n_qp, tq,
               n_groups, sink=None, diff_lambda=None, diff_subln=None, lam_init=0.0):
    bsz, _, n_qchunk, _, q_chunk = q_cols.shape
    t_q = n_qchunk * q_chunk
    per_chunk = q_chunk // tq
    c_len = rows_c.shape[2]
    use_x = rows_x is not None
    diff = diff_lambda is not None
    m_cols = 2 * n_qp * tq
    qp_blk = q_col0 // n_qp
    n_tiles = t_q // tq

    def q_index(b, g, i):
        return b, qp_blk + g, i // per_chunk, 0, i % per_chunk

    def next_step(b, g, i):
        wrap_i = (i + 1 == n_tiles).astype(jnp.int32)
        wrap_g = wrap_i * (g + 1 == n_groups).astype(jnp.int32)
        last = wrap_g * (b + 1 == bsz).astype(jnp.int32)
        return (b + wrap_g - last, (g + wrap_i) * (1 - wrap_g) + g * last, (i + 1) * (1 - wrap_i) + i * last)

    in_specs = [pl.BlockSpec((None, n_qp, None, LANES, tq), q_index)]
    args = [q_cols]
    if use_x:
        s_len = rows_x.shape[2]
        n_chunk, tk = cols_x.shape[2], cols_x.shape[4]
        in_specs += [pl.BlockSpec((None, None, s_len, LANES), lambda b, g, i: (b, k_pair0 + g, 0, 0)),
                     pl.BlockSpec((None, None, n_chunk, LANES, tk), lambda b, g, i: (b, vx_col0 + g, 0, 0, 0)),
                     pl.BlockSpec((None, n_qp, None, LANES, tq), lambda b, g, i: q_index(*next_step(b, g, i))),
                     pl.BlockSpec((None, None, tk, LANES),
                                  lambda b, g, i: (next_step(b, g, i)[0], k_pair0 + next_step(b, g, i)[1], 0, 0))]
        args += [rows_x, cols_x, q_cols, rows_x]
    in_specs += [pl.BlockSpec((None, None, c_len, LANES), lambda b, g, i: (b, k_pair0 + g, 0, 0)),
                 pl.BlockSpec((None, None, None, LANES, c_len), lambda b, g, i: (b, vc_col0 + g, 0, 0, 0))]
    args += [rows_c, cols_c]
    if sink is not None:
        in_specs.append(pl.BlockSpec(memory_space=pltpu.SMEM))
        args.append(sink)
    if diff:
        in_specs += [_const_spec((4, HEAD_DIM)), _const_spec((1, LANES))]
        args += [diff_lambda, diff_subln.reshape(1, LANES)]
    return pl.pallas_call(
        functools.partial(_full_attn_kernel, n_qp=n_qp, use_x=use_x, use_sink=sink is not None, diff=diff,
                          col_blk=min(ATTN_COL_BLOCK, m_cols), lam_init=lam_init),
        grid=(bsz, n_groups, t_q // tq),
        in_specs=in_specs,
        out_specs=pl.BlockSpec((None, n_qp, tq, LANES), lambda b, g, i: (b, g, i, 0)),
        out_shape=jax.ShapeDtypeStruct((bsz, n_groups * n_qp, t_q, LANES), BF16),
        scratch_shapes=[pltpu.VMEM((1, m_cols), F32), pltpu.VMEM((1, m_cols), F32),
                        pltpu.VMEM((LANES, m_cols), F32)]
        + ([pltpu.VMEM((cols_x.shape[4], m_cols), F32)] * 2 + [pltpu.VMEM((1, m_cols), F32)] * 2
           if use_x else []),
        compiler_params=_params(*(("arbitrary",) * 3 if use_x else ("parallel",) * 3)),
    )(*args)


def _na_kernel(qt_ref, kx_ref, vt_ref, kc_ref, vct_ref, bias_ref, o_ref, *, grid_rows):
    j = pl.program_id(2)
    tq = qt_ref.shape[1]
    n_keys = NA_SLAB * GRID_W
    row0 = jnp.clip(NA_QROWS * j - NA_KH // 2, 0, grid_rows - NA_SLAB)
    start = pl.multiple_of(row0 * GRID_W, NA_QROWS * GRID_W)
    k_slab = kx_ref[pl.ds(start, n_keys), :]
    vt_slab = _value_slab_t(vt_ref, start, n_keys)
    qt = qt_ref[...]
    top = _top_half(qt.shape)
    qs_t = jnp.concatenate([jnp.where(top, qt, jnp.zeros_like(qt)), jnp.where(top, jnp.zeros_like(qt), qt)], axis=1)
    s_loc = _dot(k_slab, qs_t) + jnp.concatenate([bias_ref[0], bias_ref[1]], axis=1)
    s_ctx = _dot(kc_ref[...], qs_t)
    m = jnp.maximum(jnp.max(s_loc, axis=0, keepdims=True), jnp.max(s_ctx, axis=0, keepdims=True))
    p_loc = jnp.exp2(s_loc - m)
    p_ctx = jnp.exp2(s_ctx - m)
    l = jnp.sum(p_loc, axis=0, keepdims=True) + jnp.sum(p_ctx, axis=0, keepdims=True)
    o_t = (_dot(vt_slab, p_loc.astype(BF16)) + _dot(vct_ref[...], p_ctx.astype(BF16))) / l
    o_ref[...] = _merge_heads_t(o_t, 0, tq).astype(o_ref.dtype)


def _na_bias_tables(na_bias, grid_rows):
    n_heads, n_dr, n_dc = na_bias.shape
    n_blk = grid_rows // NA_QROWS
    plan = []
    for j in (0, 1, n_blk - 1):
        row0 = min(max(NA_QROWS * j - NA_KH // 2, 0), grid_rows - NA_SLAB)
        rows = []
        for i in range(NA_QROWS):
            r = NA_QROWS * j + i
            r0 = min(max(r - NA_KH // 2, 0), grid_rows - NA_KH)
            rows.append(tuple((row0 + t - r + NA_KH - 1) if r0 <= row0 + t < r0 + NA_KH else None
                              for t in range(NA_SLAB)))
        plan.append(tuple(rows))
    padded = jnp.pad(jnp.flip(na_bias, axis=2), ((0, 0), (0, 16 - n_dr), (0, LANES - n_dc)))
    return pl.pallas_call(
        functools.partial(_na_bias_kernel, plan=tuple(plan)),
        grid=(n_heads,),
        in_specs=[pl.BlockSpec((None, 16, LANES), lambda h: (h, 0, 0))],
        out_specs=pl.BlockSpec((3, None, NA_SLAB * GRID_W, NA_QROWS * GRID_W), lambda h: (0, h, 0, 0)),
        out_shape=jax.ShapeDtypeStruct((3, n_heads, NA_SLAB * GRID_W, NA_QROWS * GRID_W), F32),
        compiler_params=_params("parallel"),
    )(padded)


def _na_bias_kernel(b_ref, o_ref, *, plan):
    shape = (GRID_W, LANES)
    lane = lax.broadcasted_iota(jnp.int32, shape, 1)
    kc = lax.broadcasted_iota(jnp.int32, shape, 0)
    cq = lane % GRID_W
    c0 = jnp.clip(cq - NA_KW // 2, 0, GRID_W - NA_KW)
    col_bias = jnp.where(kc >= c0, jnp.where(kc < c0 + NA_KW, 0.0, NEG), NEG).astype(F32)
    left_half = lane < GRID_W
    neg = jnp.full(shape, NEG, F32)
    used = sorted({dr for cfg in plan for row in cfg for dr in row if dr is not None})
    shift = LANES - (NA_KW - 1)
    tile_lo, tile_hi = {}, {}
    for dr in used:
        row = jnp.broadcast_to(b_ref[dr:dr + 1, :] * LOG2_E, shape)
        tile_lo[dr] = pltpu.roll(row, shift, 1, stride=1, stride_axis=0)
        tile_hi[dr] = pltpu.roll(row, (shift + GRID_W) % LANES, 1, stride=1, stride_axis=0)
    for cfg, cfg_rows in enumerate(plan):
        for t in range(NA_SLAB):
            for u in range(NA_QROWS // 2):
                dr_left, dr_right = cfg_rows[2 * u][t], cfg_rows[2 * u + 1][t]
                left = neg if dr_left is None else tile_lo[dr_left]
                right = neg if dr_right is None else tile_hi[dr_right]
                blk = jnp.where(col_bias < 0.0, neg, jnp.where(left_half, left, right))
                o_ref[cfg, t * GRID_W:(t + 1) * GRID_W, u * LANES:(u + 1) * LANES] = blk


def _na_attn(qkv_x, colx, v128_x, qkv_c, colc, bias_tabs):
    bsz, _, s_len, _ = qkv_x.shape
    c_len = qkv_c.shape[2]
    grid_rows = s_len // GRID_W
    n_blk = grid_rows // NA_QROWS
    tq = NA_QROWS * GRID_W
    n_keys = NA_SLAB * GRID_W
    per_chunk = colx.shape[4] // tq
    q0, v0, vc0 = COLS_X.index(P_BQ), COLS128_X.index(P_BV), COLS_C.index(P_BV)

    def cfg(j):
        return jnp.where(j == 0, 0, jnp.where(j == n_blk - 1, 2, 1))

    return pl.pallas_call(
        functools.partial(_na_kernel, grid_rows=grid_rows),
        grid=(bsz, 4, n_blk),
        in_specs=[pl.BlockSpec((None, None, None, LANES, tq),
                               lambda b, p, j: (b, q0 + p, j // per_chunk, 0, j % per_chunk)),
                  pl.BlockSpec((None, None, s_len, LANES), lambda b, p, j: (b, P_BK + p, 0, 0)),
                  pl.BlockSpec((None, None, s_len // LANES, LANES, LANES), lambda b, p, j: (b, v0 + p, 0, 0, 0)),
                  pl.BlockSpec((None, None, c_len, LANES), lambda b, p, j: (b, P_BK + p, 0, 0)),
                  pl.BlockSpec((None, None, None, LANES, c_len), lambda b, p, j: (b, vc0 + p, 0, 0, 0)),
                  pl.BlockSpec((None, 2, n_keys, tq), lambda b, p, j: (cfg(j), p, 0, 0))],
        out_specs=pl.BlockSpec((None, None, tq, LANES), lambda b, p, j: (b, p, j, 0)),
        out_shape=jax.ShapeDtypeStruct((bsz, 4, s_len, LANES), BF16),
        compiler_params=_params("parallel", "parallel", "parallel"),
    )(colx, qkv_x, v128_x, qkv_c, colc, bias_tabs)


def _swa_kernel(qt_ref, kx_ref, vt_ref, kc_ref, vct_ref, sink_ref, o_ref):
    n_qp, _, tq = qt_ref.shape
    n_heads = 2 * n_qp
    s_len = kx_ref.shape[0]
    n_keys = tq + 2 * SWA_WINDOW
    q0 = pl.program_id(2) * tq
    start = pl.multiple_of(jnp.clip(q0 - SWA_WINDOW, 0, s_len - n_keys), SWA_WINDOW)
    k_slab = kx_ref[pl.ds(start, n_keys), :]
    vt_slab = _value_slab_t(vt_ref, start, n_keys)
    qs_t = _stack_heads_t(qt_ref, n_qp)
    kpos = start + lax.broadcasted_iota(jnp.int32, (n_keys, tq), 0)
    qpos = q0 + lax.broadcasted_iota(jnp.int32, (n_keys, tq), 1)
    band = jnp.where(jnp.abs(kpos - qpos) <= SWA_WINDOW, 0.0, NEG).astype(F32)
    s_loc = _dot(k_slab, qs_t) + jnp.concatenate([band] * n_heads, axis=1)
    s_ctx = _dot(kc_ref[...], qs_t)
    sink = _sink_row(sink_ref, pl.program_id(1), n_heads, tq)
    m = jnp.maximum(jnp.maximum(jnp.max(s_loc, axis=0, keepdims=True), jnp.max(s_ctx, axis=0, keepdims=True)),
                    sink)
    p_loc = jnp.exp2(s_loc - m)
    p_ctx = jnp.exp2(s_ctx - m)
    l = jnp.sum(p_loc, axis=0, keepdims=True) + jnp.sum(p_ctx, axis=0, keepdims=True) + jnp.exp2(sink - m)
    o_t = (_dot(vt_slab, p_loc.astype(BF16)) + _dot(vct_ref[...], p_ctx.astype(BF16))) / l
    for p in range(n_qp):
        o_ref[p] = _merge_heads_t(o_t, p, tq).astype(o_ref.dtype)


def _swa_attn(qkv_x, colx, v128_x, qkv_c, colc, sink):
    bsz, _, s_len, _ = qkv_x.shape
    c_len = qkv_c.shape[2]
    n_qp = 2
    tq = min(SWA_TQ, s_len)
    per_chunk = colx.shape[4] // tq
    q0, v0, vc0 = COLS_X.index(P_DQ) // n_qp, COLS128_X.index(P_DV), COLS_C.index(P_DV)
    return pl.pallas_call(
        _swa_kernel,
        grid=(bsz, 2, s_len // tq),
        in_specs=[pl.BlockSpec((None, n_qp, None, LANES, tq),
                               lambda b, g, i: (b, q0 + g, i // per_chunk, 0, i % per_chunk)),
                  pl.BlockSpec((None, None, s_len, LANES), lambda b, g, i: (b, P_DK + g, 0, 0)),
                  pl.BlockSpec((None, None, s_len // LANES, LANES, LANES), lambda b, g, i: (b, v0 + g, 0, 0, 0)),
                  pl.BlockSpec((None, None, c_len, LANES), lambda b, g, i: (b, P_DK + g, 0, 0)),
                  pl.BlockSpec((None, None, None, LANES, c_len), lambda b, g, i: (b, vc0 + g, 0, 0, 0)),
                  pl.BlockSpec(memory_space=pltpu.SMEM)],
        out_specs=pl.BlockSpec((None, n_qp, tq, LANES), lambda b, g, i: (b, g, i, 0)),
        out_shape=jax.ShapeDtypeStruct((bsz, 2 * n_qp, s_len, LANES), BF16),
        compiler_params=_params("parallel", "parallel", "parallel"),
    )(colx, qkv_x, v128_x, qkv_c, colc, sink)


def _merge_kernel(h_ref, mod_ref, g_ref, ya_ref, yb_ref, yc_ref, yd_ref, wgate_ref, wbr_ref, wout_ref, o_ref):
    h = h_ref[...]
    mod = mod_ref[...]
    g = g_ref[...]
    d = h.shape[1]
    u = (_rms(h) * g[2:3] * (1.0 + mod[4:5]) + mod[3:4]).astype(BF16)
    acc = None
    for i, y_ref in enumerate((ya_ref, yb_ref, yc_ref, yd_ref)):
        y = jnp.concatenate([y_ref[p] for p in range(y_ref.shape[0])], axis=-1)
        term = _sigmoid(_dot(u, wgate_ref[:, i * d:(i + 1) * d])) * _dot(y, wbr_ref[i])
        acc = term if acc is None else acc + term
    out = _dot(acc.astype(BF16), wout_ref[...])
    o_ref[...] = h + mod[5:6] * (_rms(out) * g[3:4])


def _merge(h, mod, gains, ys, w_gate, w_branch, w_out, *, mod_row, tm):
    bsz, t, d = h.shape
    n_yp = ys[0].shape[1]
    y_spec = pl.BlockSpec((None, n_yp, tm, LANES), lambda b, i: (b, 0, i, 0))
    return pl.pallas_call(
        _merge_kernel,
        grid=(bsz, t // tm),
        in_specs=[pl.BlockSpec((None, tm, d), lambda b, i: (b, i, 0)),
                  pl.BlockSpec((None, N_MOD, d), lambda b, i: (mod_row(b), 0, 0)),
                  _const_spec((6, d)), y_spec, y_spec, y_spec, y_spec,
                  _const_spec(w_gate.shape), _const_spec(w_branch.shape), _const_spec(w_out.shape)],
        out_specs=pl.BlockSpec((None, tm, d), lambda b, i: (b, i, 0)),
        out_shape=jax.ShapeDtypeStruct((bsz, t, d), F32),
        compiler_params=_params("parallel", "parallel"),
    )(h, mod, gains, *ys, w_gate, w_branch, w_out)


def _rope_tables(n_tok):
    t = jnp.arange(n_tok, dtype=jnp.int32)
    row = (t // GRID_W).astype(F32)
    col = (t % GRID_W).astype(F32)
    n_freq = HEAD_DIM // 4
    inv_freq = ROPE_THETA ** (-jnp.arange(n_freq, dtype=F32) / n_freq)
    lane = jnp.arange(LANES, dtype=jnp.int32)
    axis = (lane % HEAD_DIM) // (HEAD_DIM // 2)
    freq = inv_freq[lane % n_freq]
    ang = jnp.where(axis[None, :] == 0, row[:, None], col[:, None]) * freq[None, :]
    first = (lane % (HEAD_DIM // 2)) < n_freq
    cos, sin = jnp.cos(ang), jnp.sin(ang)
    return cos, jnp.where(first[None, :], -sin, 0.0), jnp.where(first[None, :], 0.0, sin)


def _dup_heads(w):
    h0, h1 = w[:, :HEAD_DIM], w[:, HEAD_DIM:]
    return jnp.concatenate([h0, h0, h1, h1], axis=1)


def _qkv_weight(w_in):
    o = 0
    cols = []
    for width in (1536, 1536):
        cols.append(w_in[:, o:o + width])
        o += width
    for _ in range(2):
        cols += [w_in[:, o:o + 512], _dup_heads(w_in[:, o + 512:o + 640]), _dup_heads(w_in[:, o + 640:o + 768])]
        o += 768
    return jnp.concatenate(cols, axis=1).astype(BF16), o


def kernel(x, c, ctx, c_ctx, w_mod, b_mod, norm_gain, ffn_w_gate, ffn_w_up, ffn_w_down, w_in, w_branch,
           w_out, diff_lambda, diff_subln, na_bias, qk_norm, sink):
    bsz, s_len, d = x.shape
    c_len = ctx.shape[1]
    depth = w_mod.shape[0]
    tm = min(512, s_len)
    cond_t = jnp.zeros((d, 8), F32).at[:, :bsz].set(c.T).at[:, bsz].set(c_ctx)
    rope_tabs = _rope_tables(s_len)
    grp = jnp.arange(LANES, dtype=jnp.int32) // HEAD_DIM
    gmat = jnp.where(grp[:, None] == grp[None, :], 1.0 / HEAD_DIM, 0.0).astype(BF16)
    x_row = lambda b: b
    c_row = lambda b: bsz
    cx, cc = COLS_X.index, COLS_C.index

    h_x, h_c = x, ctx
    for l in range(depth):
        last = l == depth - 1
        lam_init = 0.8 - 0.6 * math.exp(-0.3 * l)
        mod = _modulation(cond_t, w_mod[l], b_mod[l], bsz + 1).reshape(bsz + 1, N_MOD, d)
        g = norm_gain[l]
        wg, wu, wd = ffn_w_gate[l].astype(BF16), ffn_w_up[l].astype(BF16), ffn_w_down[l].astype(BF16)
        w_qkv, gate_off = _qkv_weight(w_in[l])
        w_gate = w_in[l][:, gate_off:].astype(BF16)
        w_br = w_branch[l].astype(BF16)
        w_o = w_out[l].astype(BF16)
        qk_gain = jnp.tile(qk_norm[l], (1, 2))

        h_x = _ffn(h_x, mod, g, wg[0], wu[0], wd[0], k=0, mod_row=x_row, tm=tm)
        h_c = _ffn(h_c, mod, g, wg[0], wu[0], wd[0], k=0, mod_row=c_row, tm=c_len)

        qkv_x, colx, v128_x = _inproj(h_x, mod, g, w_qkv, qk_gain, gmat, rope_tabs, mod_row=x_row, tm=tm,
                                      cols=COLS_X, cols128=COLS128_X)
        qkv_c, colc = _inproj(h_c, mod, g, w_qkv, qk_gain, gmat, None, mod_row=c_row, tm=c_len, cols=COLS_C)

        ya = _full_attn(colx, cx(P_AQ), qkv_x, colx, qkv_c, colc, P_AK, cx(P_AV), cc(P_AV), n_qp=1,
                        tq=min(512, s_len), n_groups=4,
                        diff_lambda=diff_lambda[l], diff_subln=diff_subln[l], lam_init=lam_init)
        yb = _na_attn(qkv_x, colx, v128_x, qkv_c, colc, _na_bias_tables(na_bias[l], s_len // GRID_W))
        yc = _full_attn(colx, cx(P_CQ), qkv_x, colx, qkv_c, colc, P_CK, cx(P_CV), cc(P_CV), n_qp=2,
                        tq=min(256, s_len), n_groups=2)
        yd = _swa_attn(qkv_x, colx, v128_x, qkv_c, colc, sink[l])
        h_x = _merge(h_x, mod, g, (ya, yb, yc, yd), w_gate, w_br, w_o, mod_row=x_row, tm=tm)
        h_x = _ffn(h_x, mod, g, wg[1], wu[1], wd[1], k=2, mod_row=x_row, tm=tm)

        if not last:
            ya = _full_attn(colc, cc(P_AQ), None, None, qkv_c, colc, P_AK, None, cc(P_AV), n_qp=1, tq=c_len,
                            n_groups=4, diff_lambda=diff_lambda[l], diff_subln=diff_subln[l], lam_init=lam_init)
            yb = _full_attn(colc, cc(P_BQ), None, None, qkv_c, colc, P_BK, None, cc(P_BV), n_qp=1, tq=c_len,
                            n_groups=4)
            yc = _full_attn(colc, cc(P_CQ), None, None, qkv_c, colc, P_CK, None, cc(P_CV), n_qp=2, tq=c_len,
                            n_groups=2)
            yd = _full_attn(colc, cc(P_DQ), None, None, qkv_c, colc, P_DK, None, cc(P_DV), n_qp=2, tq=c_len,
                            n_groups=2, sink=sink[l])
            h_c = _merge(h_c, mod, g, (ya, yb, yc, yd), w_gate, w_br, w_o, mod_row=c_row, tm=c_len)
            h_c = _ffn(h_c, mod, g, wg[1], wu[1], wd[1], k=2, mod_row=c_row, tm=c_len)
    return h_x
```

```python
import functools
import math

import jax
import jax.numpy as jnp
from jax import lax
from jax.experimental import pallas as pl
from jax.experimental.pallas import tpu as pltpu

HEAD_DIM = 64
LANES = 128
GRID_W = 64
NA_KH = 8
NA_KW = 16
NA_QROWS = 4
NA_SLAB = NA_QROWS + NA_KH
SWA_WINDOW = 128
SWA_TQ = 256
KV_CHUNK = 1024
ATTN_COL_BLOCK = 512
ATTN_PAIR_UNROLL = 1
FFN_RES_WEIGHT = 0.5
ROPE_THETA = 10000.0
NORM_EPS = 1e-6
N_MOD = 9
NEG = -1e30
VMEM_LIMIT = 48 * 1024 * 1024
BF16 = jnp.bfloat16
F32 = jnp.float32

P_AQ, P_AK, P_AV = 0, 4, 8
P_BQ, P_BK, P_BV = 12, 16, 20
P_CQ, P_CK, P_CV = 24, 28, 30
P_DQ, P_DK, P_DV = 32, 36, 38
N_PAIRS = 40


def _pairs(first, n):
    return tuple(range(first, first + n))


COLS_X = _pairs(P_AQ, 4) + _pairs(P_AV, 4) + _pairs(P_CQ, 4) + _pairs(P_CV, 2) + _pairs(P_BQ, 4) + _pairs(P_DQ, 4)
COLS_C = COLS_X + _pairs(P_BV, 4) + _pairs(P_DV, 2)
COLS128_X = _pairs(P_BV, 4) + _pairs(P_DV, 2)
SEGMENTS = (
    (P_AQ, 4, None, True, True), (P_AK, 4, None, True, False), (P_AV, 4, None, False, False),
    (P_BQ, 4, None, False, True), (P_BK, 4, None, False, False), (P_BV, 4, None, False, False),
    (P_CQ, 4, 0, True, True), (P_CK, 2, 1, True, False), (P_CV, 2, None, False, False),
    (P_DQ, 4, None, True, True), (P_DK, 2, None, True, False), (P_DV, 2, None, False, False),
)
LOG2_E = math.log2(math.e)
Q_SCALE = HEAD_DIM ** -0.5 * LOG2_E


def _const_spec(shape):
    zeros = (0,) * len(shape)
    return pl.BlockSpec(shape, lambda *_: zeros, pipeline_mode=pl.Buffered(1))


def _params(*sem):
    return pltpu.CompilerParams(dimension_semantics=sem, vmem_limit_bytes=VMEM_LIMIT)


def _rms(x):
    return x * lax.rsqrt(jnp.mean(x * x, axis=-1, keepdims=True) + NORM_EPS)


def _sigmoid(x):
    return 1.0 / (1.0 + jnp.exp(-x))


def _dot(a, b):
    return jnp.dot(a, b, preferred_element_type=F32)


def _mod_kernel(ct_ref, w_ref, b_ref, o_ref, *, n_rows):
    ct = ct_ref[...]
    s = ct * _sigmoid(ct)
    w = w_ref[...]
    for r in range(n_rows):
        o_ref[r:r + 1, :] = jnp.sum(w * s[:, r:r + 1], axis=0, keepdims=True) + b_ref[...]


def _modulation(cond_t, w_mod, b_mod, n_rows):
    d, n = w_mod.shape
    tn = 1024
    return pl.pallas_call(
        functools.partial(_mod_kernel, n_rows=n_rows),
        grid=(n // tn,),
        in_specs=[pl.BlockSpec((d, 8), lambda j: (0, 0)),
                  pl.BlockSpec((d, tn), lambda j: (0, j)),
                  pl.BlockSpec((1, tn), lambda j: (0, j))],
        out_specs=pl.BlockSpec((n_rows, tn), lambda j: (0, j)),
        out_shape=jax.ShapeDtypeStruct((n_rows, n), F32),
        compiler_params=_params("parallel"),
    )(cond_t, w_mod, b_mod.reshape(1, n))


def _ffn_kernel(h_ref, mod_ref, g_ref, wg_ref, wu_ref, wd_ref, o_ref, *, k, n_chunks):
    h = h_ref[...]
    mod = mod_ref[...]
    g = g_ref[...]
    u = (_rms(h) * g[2 * k:2 * k + 1] * (1.0 + mod[3 * k + 1:3 * k + 2]) + mod[3 * k:3 * k + 1]).astype(BF16)
    cf = wg_ref.shape[1] // n_chunks
    y = None
    for c in range(n_chunks):
        gate = _dot(u, wg_ref[:, c * cf:(c + 1) * cf])
        up = _dot(u, wu_ref[:, c * cf:(c + 1) * cf])
        act = (gate * _sigmoid(gate) * up).astype(BF16)
        part = _dot(act, wd_ref[c * cf:(c + 1) * cf, :])
        y = part if y is None else y + part
    o_ref[...] = h + FFN_RES_WEIGHT * mod[3 * k + 2:3 * k + 3] * (_rms(y) * g[2 * k + 1:2 * k + 2])


def _ffn(h, mod, gains, wg, wu, wd, *, k, mod_row, tm):
    bsz, t, d = h.shape
    f = wg.shape[1]
    return pl.pallas_call(
        functools.partial(_ffn_kernel, k=k, n_chunks=2),
        grid=(bsz, t // tm),
        in_specs=[pl.BlockSpec((None, tm, d), lambda b, i: (b, i, 0)),
                  pl.BlockSpec((None, N_MOD, d), lambda b, i: (mod_row(b), 0, 0)),
                  _const_spec((6, d)), _const_spec((d, f)), _const_spec((d, f)), _const_spec((f, d))],
        out_specs=pl.BlockSpec((None, tm, d), lambda b, i: (b, i, 0)),
        out_shape=jax.ShapeDtypeStruct((bsz, t, d), F32),
        compiler_params=_params("parallel", "parallel"),
    )(h, mod, gains, wg, wu, wd)


def _inproj_kernel(*refs, rope, cols, cols128):
    refs = list(refs)
    h_ref, mod_ref, g_ref, w_ref, qkg_ref, gm_ref = refs[:6]
    cos_ref, sa_ref, sb_ref = refs[6:9] if rope else (None, None, None)
    o_ref, t_ref = refs[9:11] if rope else refs[6:8]
    t128_ref = refs[-1] if cols128 else None
    h = h_ref[...]
    mod = mod_ref[...]
    u = (_rms(h) * g_ref[2:3, :] * (1.0 + mod[4:5]) + mod[3:4]).astype(BF16)
    for p0, n_p, norm_row, rotary, scaled in SEGMENTS:
        r = _dot(u, w_ref[:, p0 * LANES:(p0 + n_p) * LANES])
        for j in range(n_p):
            x = r[:, j * LANES:(j + 1) * LANES]
            if norm_row is not None:
                xx = x * x
                hi = xx.astype(BF16)
                lo = (xx - hi.astype(F32)).astype(BF16)
                ms = _dot(hi, gm_ref[...]) + _dot(lo, gm_ref[...])
                x = x * lax.rsqrt(ms + NORM_EPS) * qkg_ref[norm_row:norm_row + 1, :]
            if rotary and rope:
                x = (x * cos_ref[...] + pltpu.roll(x, LANES - 16, 1) * sa_ref[...]
                     + pltpu.roll(x, 16, 1) * sb_ref[...])
            if scaled:
                x = x * Q_SCALE
            o_ref[p0 + j] = x.astype(BF16)
            if p0 + j in cols:
                t_ref[cols.index(p0 + j)] = x.T.astype(BF16)
            if p0 + j in cols128:
                for t in range(x.shape[0] // LANES):
                    t128_ref[cols128.index(p0 + j), t] = x[t * LANES:(t + 1) * LANES, :].T.astype(BF16)


def _inproj(h, mod, gains, w_qkv, qk_gain, gmat, rope_tabs, *, mod_row, tm, cols, cols128=()):
    bsz, t, d = h.shape
    rope = rope_tabs is not None
    chunk = min(KV_CHUNK, t)
    per_chunk = chunk // tm
    in_specs = [pl.BlockSpec((None, tm, d), lambda b, i: (b, i, 0)),
                pl.BlockSpec((None, N_MOD, d), lambda b, i: (mod_row(b), 0, 0)),
                _const_spec((6, d)), _const_spec(w_qkv.shape), _const_spec((2, LANES)),
                _const_spec((LANES, LANES))]
    args = [h, mod, gains, w_qkv, qk_gain, gmat]
    if rope:
        in_specs += [pl.BlockSpec((tm, LANES), lambda b, i: (i, 0))] * 3
        args += list(rope_tabs)
    out_specs = [pl.BlockSpec((None, N_PAIRS, tm, LANES), lambda b, i: (b, 0, i, 0)),
                 pl.BlockSpec((None, len(cols), None, LANES, tm),
                              lambda b, i: (b, 0, i // per_chunk, 0, i % per_chunk))]
    out_shape = [jax.ShapeDtypeStruct((bsz, N_PAIRS, t, LANES), BF16),
                 jax.ShapeDtypeStruct((bsz, len(cols), t // chunk, LANES, chunk), BF16)]
    if cols128:
        out_specs.append(pl.BlockSpec((None, len(cols128), tm // LANES, LANES, LANES), lambda b, i: (b, 0, i, 0, 0)))
        out_shape.append(jax.ShapeDtypeStruct((bsz, len(cols128), t // LANES, LANES, LANES), BF16))
    return pl.pallas_call(
        functools.partial(_inproj_kernel, rope=rope, cols=cols, cols128=cols128),
        grid=(bsz, t // tm),
        in_specs=in_specs,
        out_specs=out_specs,
        out_shape=out_shape,
        compiler_params=_params("parallel", "parallel"),
    )(*args)


def _top_half(shape):
    return lax.broadcasted_iota(jnp.int32, shape, 0) < HEAD_DIM


def _stack_heads_t(qt_ref, n_qp):
    parts = []
    for p in range(n_qp):
        qt = qt_ref[p]
        top = _top_half(qt.shape)
        parts += [jnp.where(top, qt, jnp.zeros_like(qt)), jnp.where(top, jnp.zeros_like(qt), qt)]
    return jnp.concatenate(parts, axis=1)


def _merge_heads_t(o_t, p, tq):
    even = o_t[:, (2 * p) * tq:(2 * p + 1) * tq]
    odd = o_t[:, (2 * p + 1) * tq:(2 * p + 2) * tq]
    return jnp.where(_top_half(even.shape), even, odd).T


def _sink_row(sink_ref, kv_head, n_heads, tq):
    col = lax.broadcasted_iota(jnp.int32, (1, n_heads * tq), 1)
    row = jnp.full((1, n_heads * tq), sink_ref[kv_head * n_heads], F32)
    for gq in range(1, n_heads):
        row = jnp.where(col >= gq * tq, sink_ref[kv_head * n_heads + gq], row)
    return row * LOG2_E


def _value_slab_t(vt_ref, start, n_keys):
    c0 = start // LANES
    return jnp.concatenate([vt_ref[c0 + t] for t in range(n_keys // LANES)], axis=1)


def _full_attn_kernel(*refs, n_qp, use_x, use_sink, diff, col_blk, lam_init):
    refs = list(refs)
    qt_ref = refs.pop(0)
    kx_ref, vxt_ref = (refs.pop(0), refs.pop(0)) if use_x else (None, None)
    qn_ref, kn_ref, kcn_ref = (refs.pop(0), refs.pop(0), refs.pop(0)) if use_x else (None, None, None)
    kc_ref, vct_ref = refs.pop(0), refs.pop(0)
    sink_ref = refs.pop(0) if use_sink else None
    dl_ref, sub_ref = (refs.pop(0), refs.pop(0)) if diff else (None, None)
    o_ref, m_ref, l_ref, acc_ref = refs[:4]
    sa_ref, sb_ref, xa_ref, xb_ref, sc_ref, xc_ref = refs[4:] if use_x else (None,) * 6
    tq = qt_ref.shape[2]
    n_heads = 2 * n_qp
    qs_t = _stack_heads_t(qt_ref, n_qp)

    if use_sink:
        m_ref[...] = _sink_row(sink_ref, pl.program_id(1), n_heads, tq)
        l_ref[...] = jnp.ones(l_ref.shape, F32)
    else:
        m_ref[...] = jnp.full(m_ref.shape, NEG, F32)
        l_ref[...] = jnp.zeros(l_ref.shape, F32)
    acc_ref[...] = jnp.zeros(acc_ref.shape, F32)

    n_blk = qs_t.shape[1] // col_blk

    def blk(c):
        return slice(c * col_blk, (c + 1) * col_blk)

    def scores(k, c, qs=qs_t):
        return _dot(k, qs[:, blk(c)])

    def softmax_pv(s, s_max, vt, c):
        m_prev = m_ref[:, blk(c)]
        m_new = jnp.maximum(m_prev, s_max)
        alpha = jnp.exp2(m_prev - m_new)
        p = jnp.exp2(s - m_new)
        l_ref[:, blk(c)] = alpha * l_ref[:, blk(c)] + jnp.sum(p, axis=0, keepdims=True)
        m_ref[:, blk(c)] = m_new
        acc_ref[:, blk(c)] = alpha * acc_ref[:, blk(c)] + _dot(vt, p.astype(BF16))

    def ctx_chunk(c):
        s = scores(kc_ref[...], c)
        softmax_pv(s, jnp.max(s, axis=0, keepdims=True), vct_ref[...], c)

    if not use_x:
        for c in range(n_blk):
            ctx_chunk(c)
    else:
        n_chunk, _, tk = vxt_ref.shape
        buf_a, buf_b = (sa_ref, xa_ref), (sb_ref, xb_ref)

        def buffers(j):
            return (buf_a, buf_b) if j % 2 == 0 else (buf_b, buf_a)

        def k_chunk(j):
            return kx_ref[pl.ds(pl.multiple_of(j * tk, tk), tk), :]

        def scores_into(buf, k, c, qs=qs_t):
            s = scores(k, c, qs)
            buf[0][:, blk(c)] = s
            buf[1][:, blk(c)] = jnp.max(s, axis=0, keepdims=True)

        buf_ctx = (sc_ref, xc_ref)

        def half_step(j, last=False):
            cur, nxt = buffers(j)
            vt = vxt_ref[j]
            qs_next = _stack_heads_t(qn_ref, n_qp) if last else qs_t
            k_next = kn_ref[...] if last else k_chunk(j + 1)
            for c in range(n_blk):
                scores_into(nxt, k_next, c, qs_next)
                if j == 0:
                    softmax_pv(sc_ref[:, blk(c)], xc_ref[:, blk(c)], vct_ref[...], c)
                softmax_pv(cur[0][:, blk(c)], cur[1][:, blk(c)], vt, c)
                if last:
                    scores_into(buf_ctx, kcn_ref[...], c, qs_next)

        first_step = (pl.program_id(0) + pl.program_id(1) + pl.program_id(2)) == 0

        @pl.when(first_step)
        def _():
            k0 = k_chunk(0)
            for c in range(n_blk):
                scores_into(buf_a, k0, c)
                scores_into(buf_ctx, kc_ref[...], c)

        unroll = 2 * ATTN_PAIR_UNROLL
        n_loop = max(n_chunk - 2, 0) // unroll * unroll
        half_step(0, last=n_chunk == 1)

        def body(jj, carry):
            for u in range(unroll):
                cur, nxt = buffers(1 + u)
                j = 1 + unroll * jj + u
                vt = vxt_ref[j]
                k_next = k_chunk(j + 1)
                for c in range(n_blk):
                    scores_into(nxt, k_next, c)
                    softmax_pv(cur[0][:, blk(c)], cur[1][:, blk(c)], vt, c)
            return carry
        lax.fori_loop(0, n_loop // unroll, body, 0)
        for j in range(1 + n_loop, n_chunk - 1):
            half_step(j)
        if n_chunk > 1:
            half_step(n_chunk - 1, last=True)
        if n_chunk % 2:
            sa_ref[...] = sb_ref[...]
            xa_ref[...] = xb_ref[...]

    o_t = acc_ref[...] / l_ref[...]
    if diff:
        dl = dl_ref[...]
        lam = (jnp.exp(jnp.sum(dl[0:1] * dl[1:2], axis=-1, keepdims=True))
               - jnp.exp(jnp.sum(dl[2:3] * dl[3:4], axis=-1, keepdims=True)) + lam_init)
        d = (o_t[:, :tq] - lam * o_t[:, tq:]).T
        o_ref[0] = (_rms(d) * sub_ref[...] * (1.0 - lam_init)).astype(o_ref.dtype)
    else:
        for p in range(n_qp):
            o_ref[p] = _merge_heads_t(o_t, p, tq).astype(o_ref.dtype)


def _full_attn(q_cols, q_col0, rows_x, cols_x, rows_c, cols_c, k_pair0, vx_col0, vc_col0, *, n_qp, tq,
               n_groups, sink=None, diff_lambda=None, diff_subln=None, lam_init=0.0):
    bsz, _, n_qchunk, _, q_chunk = q_cols.shape
    t_q = n_qchunk * q_chunk
    per_chunk = q_chunk // tq
    c_len = rows_c.shape[2]
    use_x = rows_x is not None
    diff = diff_lambda is not None
    m_cols = 2 * n_qp * tq
    qp_blk = q_col0 // n_qp
    n_tiles = t_q // tq

    def q_index(b, g, i):
        return b, qp_blk + g, i // per_chunk, 0, i % per_chunk

    def next_step(b, g, i):
        wrap_i = (i + 1 == n_tiles).astype(jnp.int32)
        wrap_g = wrap_i * (g + 1 == n_groups).astype(jnp.int32)
        last = wrap_g * (b + 1 == bsz).astype(jnp.int32)
        return (b + wrap_g - last, (g + wrap_i) * (1 - wrap_g) + g * last, (i + 1) * (1 - wrap_i) + i * last)

    in_specs = [pl.BlockSpec((None, n_qp, None, LANES, tq), q_index)]
    args = [q_cols]
    if use_x:
        s_len = rows_x.shape[2]
        n_chunk, tk = cols_x.shape[2], cols_x.shape[4]
        in_specs += [pl.BlockSpec((None, None, s_len, LANES), lambda b, g, i: (b, k_pair0 + g, 0, 0)),
                     pl.BlockSpec((None, None, n_chunk, LANES, tk), lambda b, g, i: (b, vx_col0 + g, 0, 0, 0)),
                     pl.BlockSpec((None, n_qp, None, LANES, tq), lambda b, g, i: q_index(*next_step(b, g, i))),
                     pl.BlockSpec((None, None, tk, LANES),
                                  lambda b, g, i: (next_step(b, g, i)[0], k_pair0 + next_step(b, g, i)[1], 0, 0)),
                     pl.BlockSpec((None, None, c_len, LANES),
                                  lambda b, g, i: (next_step(b, g, i)[0], k_pair0 + next_step(b, g, i)[1], 0, 0))]
        args += [rows_x, cols_x, q_cols, rows_x, rows_c]
    in_specs += [pl.BlockSpec((None, None, c_len, LANES), lambda b, g, i: (b, k_pair0 + g, 0, 0)),
                 pl.BlockSpec((None, None, None, LANES, c_len), lambda b, g, i: (b, vc_col0 + g, 0, 0, 0))]
    args += [rows_c, cols_c]
    if sink is not None:
        in_specs.append(pl.BlockSpec(memory_space=pltpu.SMEM))
        args.append(sink)
    if diff:
        in_specs += [_const_spec((4, HEAD_DIM)), _const_spec((1, LANES))]
        args += [diff_lambda, diff_subln.reshape(1, LANES)]
    return pl.pallas_call(
        functools.partial(_full_attn_kernel, n_qp=n_qp, use_x=use_x, use_sink=sink is not None, diff=diff,
                          col_blk=min(ATTN_COL_BLOCK, m_cols), lam_init=lam_init),
        grid=(bsz, n_groups, t_q // tq),
        in_specs=in_specs,
        out_specs=pl.BlockSpec((None, n_qp, tq, LANES), lambda b, g, i: (b, g, i, 0)),
        out_shape=jax.ShapeDtypeStruct((bsz, n_groups * n_qp, t_q, LANES), BF16),
        scratch_shapes=[pltpu.VMEM((1, m_cols), F32), pltpu.VMEM((1, m_cols), F32),
                        pltpu.VMEM((LANES, m_cols), F32)]
        + ([pltpu.VMEM((cols_x.shape[4], m_cols), F32)] * 2 + [pltpu.VMEM((1, m_cols), F32)] * 2
           + [pltpu.VMEM((c_len, m_cols), F32), pltpu.VMEM((1, m_cols), F32)] if use_x else []),
        compiler_params=_params(*(("arbitrary",) * 3 if use_x else ("parallel",) * 3)),
    )(*args)


def _na_kernel(qt_ref, kx_ref, vt_ref, kc_ref, vct_ref, bias_ref, o_ref, *, grid_rows):
    j = pl.program_id(2)
    tq = qt_ref.shape[1]
    n_keys = NA_SLAB * GRID_W
    row0 = jnp.clip(NA_QROWS * j - NA_KH // 2, 0, grid_rows - NA_SLAB)
    start = pl.multiple_of(row0 * GRID_W, (NA_KH // 2) * GRID_W)
    k_slab = kx_ref[pl.ds(start, n_keys), :]
    vt_slab = _value_slab_t(vt_ref, start, n_keys)
    qt = qt_ref[...]
    top = _top_half(qt.shape)
    qs_t = jnp.concatenate([jnp.where(top, qt, jnp.zeros_like(qt)), jnp.where(top, jnp.zeros_like(qt), qt)], axis=1)
    s_loc = _dot(k_slab, qs_t) + jnp.concatenate([bias_ref[0], bias_ref[1]], axis=1)
    s_ctx = _dot(kc_ref[...], qs_t)
    m = jnp.maximum(jnp.max(s_loc, axis=0, keepdims=True), jnp.max(s_ctx, axis=0, keepdims=True))
    p_loc = jnp.exp2(s_loc - m)
    p_ctx = jnp.exp2(s_ctx - m)
    l = jnp.sum(p_loc, axis=0, keepdims=True) + jnp.sum(p_ctx, axis=0, keepdims=True)
    o_t = (_dot(vt_slab, p_loc.astype(BF16)) + _dot(vct_ref[...], p_ctx.astype(BF16))) / l
    o_ref[...] = _merge_heads_t(o_t, 0, tq).astype(o_ref.dtype)


def _na_bias_tables(na_bias, grid_rows):
    n_heads, n_dr, n_dc = na_bias.shape
    n_blk = grid_rows // NA_QROWS
    plan = []
    for j in (0, 1, n_blk - 1):
        row0 = min(max(NA_QROWS * j - NA_KH // 2, 0), grid_rows - NA_SLAB)
        rows = []
        for i in range(NA_QROWS):
            r = NA_QROWS * j + i
            r0 = min(max(r - NA_KH // 2, 0), grid_rows - NA_KH)
            rows.append(tuple((row0 + t - r + NA_KH - 1) if r0 <= row0 + t < r0 + NA_KH else None
                              for t in range(NA_SLAB)))
        plan.append(tuple(rows))
    padded = jnp.pad(jnp.flip(na_bias, axis=2), ((0, 0), (0, 16 - n_dr), (0, LANES - n_dc)))
    return pl.pallas_call(
        functools.partial(_na_bias_kernel, plan=tuple(plan)),
        grid=(n_heads,),
        in_specs=[pl.BlockSpec((None, 16, LANES), lambda h: (h, 0, 0))],
        out_specs=pl.BlockSpec((3, None, NA_SLAB * GRID_W, NA_QROWS * GRID_W), lambda h: (0, h, 0, 0)),
        out_shape=jax.ShapeDtypeStruct((3, n_heads, NA_SLAB * GRID_W, NA_QROWS * GRID_W), F32),
        compiler_params=_params("parallel"),
    )(padded)


def _na_bias_kernel(b_ref, o_ref, *, plan):
    shape = (GRID_W, LANES)
    lane = lax.broadcasted_iota(jnp.int32, shape, 1)
    kc = lax.broadcasted_iota(jnp.int32, shape, 0)
    cq = lane % GRID_W
    c0 = jnp.clip(cq - NA_KW // 2, 0, GRID_W - NA_KW)
    col_bias = jnp.where(kc >= c0, jnp.where(kc < c0 + NA_KW, 0.0, NEG), NEG).astype(F32)
    left_half = lane < GRID_W
    neg = jnp.full(shape, NEG, F32)
    used = sorted({dr for cfg in plan for row in cfg for dr in row if dr is not None})
    shift = LANES - (NA_KW - 1)
    tile_lo, tile_hi = {}, {}
    for dr in used:
        row = jnp.broadcast_to(b_ref[dr:dr + 1, :] * LOG2_E, shape)
        tile_lo[dr] = pltpu.roll(row, shift, 1, stride=1, stride_axis=0)
        tile_hi[dr] = pltpu.roll(row, (shift + GRID_W) % LANES, 1, stride=1, stride_axis=0)
    for cfg, cfg_rows in enumerate(plan):
        for t in range(NA_SLAB):
            for u in range(NA_QROWS // 2):
                dr_left, dr_right = cfg_rows[2 * u][t], cfg_rows[2 * u + 1][t]
                left = neg if dr_left is None else tile_lo[dr_left]
                right = neg if dr_right is None else tile_hi[dr_right]
                blk = jnp.where(col_bias < 0.0, neg, jnp.where(left_half, left, right))
                o_ref[cfg, t * GRID_W:(t + 1) * GRID_W, u * LANES:(u + 1) * LANES] = blk


def _na_attn(qkv_x, colx, v128_x, qkv_c, colc, bias_tabs):
    bsz, _, s_len, _ = qkv_x.shape
    c_len = qkv_c.shape[2]
    grid_rows = s_len // GRID_W
    n_blk = grid_rows // NA_QROWS
    tq = NA_QROWS * GRID_W
    n_keys = NA_SLAB * GRID_W
    per_chunk = colx.shape[4] // tq
    q0, v0, vc0 = COLS_X.index(P_BQ), COLS128_X.index(P_BV), COLS_C.index(P_BV)

    def cfg(j):
        return jnp.where(j == 0, 0, jnp.where(j == n_blk - 1, 2, 1))

    return pl.pallas_call(
        functools.partial(_na_kernel, grid_rows=grid_rows),
        grid=(bsz, 4, n_blk),
        in_specs=[pl.BlockSpec((None, None, None, LANES, tq),
                               lambda b, p, j: (b, q0 + p, j // per_chunk, 0, j % per_chunk)),
                  pl.BlockSpec((None, None, s_len, LANES), lambda b, p, j: (b, P_BK + p, 0, 0)),
                  pl.BlockSpec((None, None, s_len // LANES, LANES, LANES), lambda b, p, j: (b, v0 + p, 0, 0, 0)),
                  pl.BlockSpec((None, None, c_len, LANES), lambda b, p, j: (b, P_BK + p, 0, 0)),
                  pl.BlockSpec((None, None, None, LANES, c_len), lambda b, p, j: (b, vc0 + p, 0, 0, 0)),
                  pl.BlockSpec((None, 2, n_keys, tq), lambda b, p, j: (cfg(j), p, 0, 0))],
        out_specs=pl.BlockSpec((None, None, tq, LANES), lambda b, p, j: (b, p, j, 0)),
        out_shape=jax.ShapeDtypeStruct((bsz, 4, s_len, LANES), BF16),
        compiler_params=_params("parallel", "parallel", "parallel"),
    )(colx, qkv_x, v128_x, qkv_c, colc, bias_tabs)


def _swa_kernel(qt_ref, kx_ref, vt_ref, kc_ref, vct_ref, sink_ref, o_ref):
    n_qp, _, tq = qt_ref.shape
    n_heads = 2 * n_qp
    s_len = kx_ref.shape[0]
    n_keys = tq + 2 * SWA_WINDOW
    q0 = pl.program_id(2) * tq
    start = pl.multiple_of(jnp.clip(q0 - SWA_WINDOW, 0, s_len - n_keys), SWA_WINDOW)
    k_slab = kx_ref[pl.ds(start, n_keys), :]
    vt_slab = _value_slab_t(vt_ref, start, n_keys)
    qs_t = _stack_heads_t(qt_ref, n_qp)
    kpos = start + lax.broadcasted_iota(jnp.int32, (n_keys, tq), 0)
    qpos = q0 + lax.broadcasted_iota(jnp.int32, (n_keys, tq), 1)
    band = jnp.where(jnp.abs(kpos - qpos) <= SWA_WINDOW, 0.0, NEG).astype(F32)
    s_loc = _dot(k_slab, qs_t) + jnp.concatenate([band] * n_heads, axis=1)
    s_ctx = _dot(kc_ref[...], qs_t)
    sink = _sink_row(sink_ref, pl.program_id(1), n_heads, tq)
    m = jnp.maximum(jnp.maximum(jnp.max(s_loc, axis=0, keepdims=True), jnp.max(s_ctx, axis=0, keepdims=True)),
                    sink)
    p_loc = jnp.exp2(s_loc - m)
    p_ctx = jnp.exp2(s_ctx - m)
    l = jnp.sum(p_loc, axis=0, keepdims=True) + jnp.sum(p_ctx, axis=0, keepdims=True) + jnp.exp2(sink - m)
    o_t = (_dot(vt_slab, p_loc.astype(BF16)) + _dot(vct_ref[...], p_ctx.astype(BF16))) / l
    for p in range(n_qp):
        o_ref[p] = _merge_heads_t(o_t, p, tq).astype(o_ref.dtype)


def _swa_attn(qkv_x, colx, v128_x, qkv_c, colc, sink):
    bsz, _, s_len, _ = qkv_x.shape
    c_len = qkv_c.shape[2]
    n_qp = 2
    tq = min(SWA_TQ, s_len)
    per_chunk = colx.shape[4] // tq
    q0, v0, vc0 = COLS_X.index(P_DQ) // n_qp, COLS128_X.index(P_DV), COLS_C.index(P_DV)
    return pl.pallas_call(
        _swa_kernel,
        grid=(bsz, 2, s_len // tq),
        in_specs=[pl.BlockSpec((None, n_qp, None, LANES, tq),
                               lambda b, g, i: (b, q0 + g, i // per_chunk, 0, i % per_chunk)),
                  pl.BlockSpec((None, None, s_len, LANES), lambda b, g, i: (b, P_DK + g, 0, 0)),
                  pl.BlockSpec((None, None, s_len // LANES, LANES, LANES), lambda b, g, i: (b, v0 + g, 0, 0, 0)),
                  pl.BlockSpec((None, None, c_len, LANES), lambda b, g, i: (b, P_DK + g, 0, 0)),
                  pl.BlockSpec((None, None, None, LANES, c_len), lambda b, g, i: (b, vc0 + g, 0, 0, 0)),
                  pl.BlockSpec(memory_space=pltpu.SMEM)],
        out_specs=pl.BlockSpec((None, n_qp, tq, LANES), lambda b, g, i: (b, g, i, 0)),
        out_shape=jax.ShapeDtypeStruct((bsz, 2 * n_qp, s_len, LANES), BF16),
        compiler_params=_params("parallel", "parallel", "parallel"),
    )(colx, qkv_x, v128_x, qkv_c, colc, sink)


def _merge_kernel(h_ref, mod_ref, g_ref, ya_ref, yb_ref, yc_ref, yd_ref, wgate_ref, wbr_ref, wout_ref, o_ref):
    h = h_ref[...]
    mod = mod_ref[...]
    g = g_ref[...]
    d = h.shape[1]
    u = (_rms(h) * g[2:3] * (1.0 + mod[4:5]) + mod[3:4]).astype(BF16)
    acc = None
    for i, y_ref in enumerate((ya_ref, yb_ref, yc_ref, yd_ref)):
        y = jnp.concatenate([y_ref[p] for p in range(y_ref.shape[0])], axis=-1)
        term = _sigmoid(_dot(u, wgate_ref[:, i * d:(i + 1) * d])) * _dot(y, wbr_ref[i])
        acc = term if acc is None else acc + term
    out = _dot(acc.astype(BF16), wout_ref[...])
    o_ref[...] = h + mod[5:6] * (_rms(out) * g[3:4])


def _merge(h, mod, gains, ys, w_gate, w_branch, w_out, *, mod_row, tm):
    bsz, t, d = h.shape
    n_yp = ys[0].shape[1]
    y_spec = pl.BlockSpec((None, n_yp, tm, LANES), lambda b, i: (b, 0, i, 0))
    return pl.pallas_call(
        _merge_kernel,
        grid=(bsz, t // tm),
        in_specs=[pl.BlockSpec((None, tm, d), lambda b, i: (b, i, 0)),
                  pl.BlockSpec((None, N_MOD, d), lambda b, i: (mod_row(b), 0, 0)),
                  _const_spec((6, d)), y_spec, y_spec, y_spec, y_spec,
                  _const_spec(w_gate.shape), _const_spec(w_branch.shape), _const_spec(w_out.shape)],
        out_specs=pl.BlockSpec((None, tm, d), lambda b, i: (b, i, 0)),
        out_shape=jax.ShapeDtypeStruct((bsz, t, d), F32),
        compiler_params=_params("parallel", "parallel"),
    )(h, mod, gains, *ys, w_gate, w_branch, w_out)


def _rope_tables(n_tok):
    t = jnp.arange(n_tok, dtype=jnp.int32)
    row = (t // GRID_W).astype(F32)
    col = (t % GRID_W).astype(F32)
    n_freq = HEAD_DIM // 4
    inv_freq = ROPE_THETA ** (-jnp.arange(n_freq, dtype=F32) / n_freq)
    lane = jnp.arange(LANES, dtype=jnp.int32)
    axis = (lane % HEAD_DIM) // (HEAD_DIM // 2)
    freq = inv_freq[lane % n_freq]
    ang = jnp.where(axis[None, :] == 0, row[:, None], col[:, None]) * freq[None, :]
    first = (lane % (HEAD_DIM // 2)) < n_freq
    cos, sin = jnp.cos(ang), jnp.sin(ang)
    return cos, jnp.where(first[None, :], -sin, 0.0), jnp.where(first[None, :], 0.0, sin)


def _dup_heads(w):
    h0, h1 = w[:, :HEAD_DIM], w[:, HEAD_DIM:]
    return jnp.concatenate([h0, h0, h1, h1], axis=1)


def _qkv_weight(w_in):
    o = 0
    cols = []
    for width in (1536, 1536):
        cols.append(w_in[:, o:o + width])
        o += width
    for _ in range(2):
        cols += [w_in[:, o:o + 512], _dup_heads(w_in[:, o + 512:o + 640]), _dup_heads(w_in[:, o + 640:o + 768])]
        o += 768
    return jnp.concatenate(cols, axis=1).astype(BF16), o


def kernel(x, c, ctx, c_ctx, w_mod, b_mod, norm_gain, ffn_w_gate, ffn_w_up, ffn_w_down, w_in, w_branch,
           w_out, diff_lambda, diff_subln, na_bias, qk_norm, sink):
    bsz, s_len, d = x.shape
    c_len = ctx.shape[1]
    depth = w_mod.shape[0]
    tm = min(512, s_len)
    cond_t = jnp.zeros((d, 8), F32).at[:, :bsz].set(c.T).at[:, bsz].set(c_ctx)
    rope_tabs = _rope_tables(s_len)
    grp = jnp.arange(LANES, dtype=jnp.int32) // HEAD_DIM
    gmat = jnp.where(grp[:, None] == grp[None, :], 1.0 / HEAD_DIM, 0.0).astype(BF16)
    x_row = lambda b: b
    c_row = lambda b: bsz
    cx, cc = COLS_X.index, COLS_C.index

    h_x, h_c = x, ctx
    for l in range(depth):
        last = l == depth - 1
        lam_init = 0.8 - 0.6 * math.exp(-0.3 * l)
        mod = _modulation(cond_t, w_mod[l], b_mod[l], bsz + 1).reshape(bsz + 1, N_MOD, d)
        g = norm_gain[l]
        wg, wu, wd = ffn_w_gate[l].astype(BF16), ffn_w_up[l].astype(BF16), ffn_w_down[l].astype(BF16)
        w_qkv, gate_off = _qkv_weight(w_in[l])
        w_gate = w_in[l][:, gate_off:].astype(BF16)
        w_br = w_branch[l].astype(BF16)
        w_o = w_out[l].astype(BF16)
        qk_gain = jnp.tile(qk_norm[l], (1, 2))

        h_x = _ffn(h_x, mod, g, wg[0], wu[0], wd[0], k=0, mod_row=x_row, tm=tm)
        h_c = _ffn(h_c, mod, g, wg[0], wu[0], wd[0], k=0, mod_row=c_row, tm=c_len)

        qkv_x, colx, v128_x = _inproj(h_x, mod, g, w_qkv, qk_gain, gmat, rope_tabs, mod_row=x_row, tm=tm,
                                      cols=COLS_X, cols128=COLS128_X)
        qkv_c, colc = _inproj(h_c, mod, g, w_qkv, qk_gain, gmat, None, mod_row=c_row, tm=c_len, cols=COLS_C)

        ya = _full_attn(colx, cx(P_AQ), qkv_x, colx, qkv_c, colc, P_AK, cx(P_AV), cc(P_AV), n_qp=1,
                        tq=min(512, s_len), n_groups=4,
                        diff_lambda=diff_lambda[l], diff_subln=diff_subln[l], lam_init=lam_init)
        yb = _na_attn(qkv_x, colx, v128_x, qkv_c, colc, _na_bias_tables(na_bias[l], s_len // GRID_W))
        yc = _full_attn(colx, cx(P_CQ), qkv_x, colx, qkv_c, colc, P_CK, cx(P_CV), cc(P_CV), n_qp=2,
                        tq=min(256, s_len), n_groups=2)
        yd = _swa_attn(qkv_x, colx, v128_x, qkv_c, colc, sink[l])
        h_x = _merge(h_x, mod, g, (ya, yb, yc, yd), w_gate, w_br, w_o, mod_row=x_row, tm=tm)
        h_x = _ffn(h_x, mod, g, wg[1], wu[1], wd[1], k=2, mod_row=x_row, tm=tm)

        if not last:
            ya = _full_attn(colc, cc(P_AQ), None, None, qkv_c, colc, P_AK, None, cc(P_AV), n_qp=1, tq=c_len,
                            n_groups=4, diff_lambda=diff_lambda[l], diff_subln=diff_subln[l], lam_init=lam_init)
            yb = _full_attn(colc, cc(P_BQ), None, None, qkv_c, colc, P_BK, None, cc(P_BV), n_qp=1, tq=c_len,
                            n_groups=4)
            yc = _full_attn(colc, cc(P_CQ), None, None, qkv_c, colc, P_CK, None, cc(P_CV), n_qp=2, tq=c_len,
                            n_groups=2)
            yd = _full_attn(colc, cc(P_DQ), None, None, qkv_c, colc, P_DK, None, cc(P_DV), n_qp=2, tq=c_len,
                            n_groups=2, sink=sink[l])
            h_c = _merge(h_c, mod, g, (ya, yb, yc, yd), w_gate, w_br, w_o, mod_row=c_row, tm=c_len)
            h_c = _ffn(h_c, mod, g, wg[1], wu[1], wd[1], k=2, mod_row=c_row, tm=c_len)
    return h_x
```

```python
import functools
import math

import jax
import jax.numpy as jnp
from jax import lax
from jax.experimental import pallas as pl
from jax.experimental.pallas import tpu as pltpu

HEAD_DIM = 64
LANES = 128
GRID_W = 64
NA_KH = 8
NA_KW = 16
NA_QROWS = 4
NA_SLAB = NA_QROWS + NA_KH
SWA_WINDOW = 128
SWA_TQ = 256
KV_CHUNK = 1024
ATTN_COL_BLOCK = 256
ATTN_PAIR_UNROLL = 1
FFN_RES_WEIGHT = 0.5
ROPE_THETA = 10000.0
NORM_EPS = 1e-6
N_MOD = 9
NEG = -1e30
VMEM_LIMIT = 48 * 1024 * 1024
BF16 = jnp.bfloat16
F32 = jnp.float32

P_AQ, P_AK, P_AV = 0, 4, 8
P_BQ, P_BK, P_BV = 12, 16, 20
P_CQ, P_CK, P_CV = 24, 28, 30
P_DQ, P_DK, P_DV = 32, 36, 38
N_PAIRS = 40


def _pairs(first, n):
    return tuple(range(first, first + n))


COLS_X = _pairs(P_AQ, 4) + _pairs(P_AV, 4) + _pairs(P_CQ, 4) + _pairs(P_CV, 2) + _pairs(P_BQ, 4) + _pairs(P_DQ, 4)
COLS_C = COLS_X + _pairs(P_BV, 4) + _pairs(P_DV, 2)
COLS128_X = _pairs(P_BV, 4) + _pairs(P_DV, 2)
SEGMENTS = (
    (P_AQ, 4, None, True, True), (P_AK, 4, None, True, False), (P_AV, 4, None, False, False),
    (P_BQ, 4, None, False, True), (P_BK, 4, None, False, False), (P_BV, 4, None, False, False),
    (P_CQ, 4, 0, True, True), (P_CK, 2, 1, True, False), (P_CV, 2, None, False, False),
    (P_DQ, 4, None, True, True), (P_DK, 2, None, True, False), (P_DV, 2, None, False, False),
)
LOG2_E = math.log2(math.e)
Q_SCALE = HEAD_DIM ** -0.5 * LOG2_E


def _const_spec(shape):
    zeros = (0,) * len(shape)
    return pl.BlockSpec(shape, lambda *_: zeros, pipeline_mode=pl.Buffered(1))


def _params(*sem):
    return pltpu.CompilerParams(dimension_semantics=sem, vmem_limit_bytes=VMEM_LIMIT)


def _rms(x):
    return x * lax.rsqrt(jnp.mean(x * x, axis=-1, keepdims=True) + NORM_EPS)


def _sigmoid(x):
    return 1.0 / (1.0 + jnp.exp(-x))


def _dot(a, b):
    return jnp.dot(a, b, preferred_element_type=F32)


def _mod_kernel(ct_ref, w_ref, b_ref, o_ref, *, n_rows):
    ct = ct_ref[...]
    s = ct * _sigmoid(ct)
    w = w_ref[...]
    for r in range(n_rows):
        o_ref[r:r + 1, :] = jnp.sum(w * s[:, r:r + 1], axis=0, keepdims=True) + b_ref[...]


def _modulation(cond_t, w_mod, b_mod, n_rows):
    d, n = w_mod.shape
    tn = 1024
    return pl.pallas_call(
        functools.partial(_mod_kernel, n_rows=n_rows),
        grid=(n // tn,),
        in_specs=[pl.BlockSpec((d, 8), lambda j: (0, 0)),
                  pl.BlockSpec((d, tn), lambda j: (0, j)),
                  pl.BlockSpec((1, tn), lambda j: (0, j))],
        out_specs=pl.BlockSpec((n_rows, tn), lambda j: (0, j)),
        out_shape=jax.ShapeDtypeStruct((n_rows, n), F32),
        compiler_params=_params("parallel"),
    )(cond_t, w_mod, b_mod.reshape(1, n))


def _ffn_kernel(h_ref, mod_ref, g_ref, wg_ref, wu_ref, wd_ref, o_ref, *, k, n_chunks):
    h = h_ref[...]
    mod = mod_ref[...]
    g = g_ref[...]
    u = (_rms(h) * g[2 * k:2 * k + 1] * (1.0 + mod[3 * k + 1:3 * k + 2]) + mod[3 * k:3 * k + 1]).astype(BF16)
    cf = wg_ref.shape[1] // n_chunks
    y = None
    for c in range(n_chunks):
        gate = _dot(u, wg_ref[:, c * cf:(c + 1) * cf])
        up = _dot(u, wu_ref[:, c * cf:(c + 1) * cf])
        act = (gate * _sigmoid(gate) * up).astype(BF16)
        part = _dot(act, wd_ref[c * cf:(c + 1) * cf, :])
        y = part if y is None else y + part
    o_ref[...] = h + FFN_RES_WEIGHT * mod[3 * k + 2:3 * k + 3] * (_rms(y) * g[2 * k + 1:2 * k + 2])


def _ffn(h, mod, gains, wg, wu, wd, *, k, mod_row, tm):
    bsz, t, d = h.shape
    f = wg.shape[1]
    return pl.pallas_call(
        functools.partial(_ffn_kernel, k=k, n_chunks=2),
        grid=(bsz, t // tm),
        in_specs=[pl.BlockSpec((None, tm, d), lambda b, i: (b, i, 0)),
                  pl.BlockSpec((None, N_MOD, d), lambda b, i: (mod_row(b), 0, 0)),
                  _const_spec((6, d)), _const_spec((d, f)), _const_spec((d, f)), _const_spec((f, d))],
        out_specs=pl.BlockSpec((None, tm, d), lambda b, i: (b, i, 0)),
        out_shape=jax.ShapeDtypeStruct((bsz, t, d), F32),
        compiler_params=_params("parallel", "parallel"),
    )(h, mod, gains, wg, wu, wd)


def _inproj_kernel(*refs, rope, cols, cols128):
    refs = list(refs)
    h_ref, mod_ref, g_ref, w_ref, qkg_ref, gm_ref = refs[:6]
    cos_ref, sa_ref, sb_ref = refs[6:9] if rope else (None, None, None)
    o_ref, t_ref = refs[9:11] if rope else refs[6:8]
    t128_ref = refs[-1] if cols128 else None
    h = h_ref[...]
    mod = mod_ref[...]
    u = (_rms(h) * g_ref[2:3, :] * (1.0 + mod[4:5]) + mod[3:4]).astype(BF16)
    for p0, n_p, norm_row, rotary, scaled in SEGMENTS:
        r = _dot(u, w_ref[:, p0 * LANES:(p0 + n_p) * LANES])
        for j in range(n_p):
            x = r[:, j * LANES:(j + 1) * LANES]
            if norm_row is not None:
                xx = x * x
                hi = xx.astype(BF16)
                lo = (xx - hi.astype(F32)).astype(BF16)
                ms = _dot(hi, gm_ref[...]) + _dot(lo, gm_ref[...])
                x = x * lax.rsqrt(ms + NORM_EPS) * qkg_ref[norm_row:norm_row + 1, :]
            if rotary and rope:
                x = (x * cos_ref[...] + pltpu.roll(x, LANES - 16, 1) * sa_ref[...]
                     + pltpu.roll(x, 16, 1) * sb_ref[...])
            if scaled:
                x = x * Q_SCALE
            o_ref[p0 + j] = x.astype(BF16)
            if p0 + j in cols:
                t_ref[cols.index(p0 + j)] = x.T.astype(BF16)
            if p0 + j in cols128:
                for t in range(x.shape[0] // LANES):
                    t128_ref[cols128.index(p0 + j), t] = x[t * LANES:(t + 1) * LANES, :].T.astype(BF16)


def _inproj(h, mod, gains, w_qkv, qk_gain, gmat, rope_tabs, *, mod_row, tm, cols, cols128=()):
    bsz, t, d = h.shape
    rope = rope_tabs is not None
    chunk = min(KV_CHUNK, t)
    per_chunk = chunk // tm
    in_specs = [pl.BlockSpec((None, tm, d), lambda b, i: (b, i, 0)),
                pl.BlockSpec((None, N_MOD, d), lambda b, i: (mod_row(b), 0, 0)),
                _const_spec((6, d)), _const_spec(w_qkv.shape), _const_spec((2, LANES)),
                _const_spec((LANES, LANES))]
    args = [h, mod, gains, w_qkv, qk_gain, gmat]
    if rope:
        in_specs += [pl.BlockSpec((tm, LANES), lambda b, i: (i, 0))] * 3
        args += list(rope_tabs)
    out_specs = [pl.BlockSpec((None, N_PAIRS, tm, LANES), lambda b, i: (b, 0, i, 0)),
                 pl.BlockSpec((None, len(cols), None, LANES, tm),
                              lambda b, i: (b, 0, i // per_chunk, 0, i % per_chunk))]
    out_shape = [jax.ShapeDtypeStruct((bsz, N_PAIRS, t, LANES), BF16),
                 jax.ShapeDtypeStruct((bsz, len(cols), t // chunk, LANES, chunk), BF16)]
    if cols128:
        out_specs.append(pl.BlockSpec((None, len(cols128), tm // LANES, LANES, LANES), lambda b, i: (b, 0, i, 0, 0)))
        out_shape.append(jax.ShapeDtypeStruct((bsz, len(cols128), t // LANES, LANES, LANES), BF16))
    return pl.pallas_call(
        functools.partial(_inproj_kernel, rope=rope, cols=cols, cols128=cols128),
        grid=(bsz, t // tm),
        in_specs=in_specs,
        out_specs=out_specs,
        out_shape=out_shape,
        compiler_params=_params("parallel", "parallel"),
    )(*args)


def _top_half(shape):
    return lax.broadcasted_iota(jnp.int32, shape, 0) < HEAD_DIM


def _stack_heads_t(qt_ref, n_qp):
    parts = []
    for p in range(n_qp):
        qt = qt_ref[p]
        top = _top_half(qt.shape)
        parts += [jnp.where(top, qt, jnp.zeros_like(qt)), jnp.where(top, jnp.zeros_like(qt), qt)]
    return jnp.concatenate(parts, axis=1)


def _merge_heads_t(o_t, p, tq):
    even = o_t[:, (2 * p) * tq:(2 * p + 1) * tq]
    odd = o_t[:, (2 * p + 1) * tq:(2 * p + 2) * tq]
    return jnp.where(_top_half(even.shape), even, odd).T


def _sink_row(sink_ref, kv_head, n_heads, tq):
    col = lax.broadcasted_iota(jnp.int32, (1, n_heads * tq), 1)
    row = jnp.full((1, n_heads * tq), sink_ref[kv_head * n_heads], F32)
    for gq in range(1, n_heads):
        row = jnp.where(col >= gq * tq, sink_ref[kv_head * n_heads + gq], row)
    return row * LOG2_E


def _value_slab_t(vt_ref, start, n_keys):
    c0 = start // LANES
    return jnp.concatenate([vt_ref[c0 + t] for t in range(n_keys // LANES)], axis=1)


def _full_attn_kernel(*refs, n_qp, use_x, use_sink, diff, col_blk, lam_init):
    refs = list(refs)
    qt_ref = refs.pop(0)
    kx_ref, vxt_ref = (refs.pop(0), refs.pop(0)) if use_x else (None, None)
    qn_ref, kn_ref, kcn_ref = (refs.pop(0), refs.pop(0), refs.pop(0)) if use_x else (None, None, None)
    kc_ref, vct_ref = refs.pop(0), refs.pop(0)
    sink_ref = refs.pop(0) if use_sink else None
    dl_ref, sub_ref = (refs.pop(0), refs.pop(0)) if diff else (None, None)
    o_ref, m_ref, l_ref, acc_ref = refs[:4]
    sa_ref, sb_ref, xa_ref, xb_ref, sc_ref, xc_ref = refs[4:] if use_x else (None,) * 6
    tq = qt_ref.shape[2]
    n_heads = 2 * n_qp
    qs_t = _stack_heads_t(qt_ref, n_qp)

    if use_sink:
        m_ref[...] = _sink_row(sink_ref, pl.program_id(1), n_heads, tq)
        l_ref[...] = jnp.ones(l_ref.shape, F32)
    else:
        m_ref[...] = jnp.full(m_ref.shape, NEG, F32)
        l_ref[...] = jnp.zeros(l_ref.shape, F32)
    acc_ref[...] = jnp.zeros(acc_ref.shape, F32)

    n_blk = qs_t.shape[1] // col_blk

    def blk(c):
        return slice(c * col_blk, (c + 1) * col_blk)

    def scores(k, c, qs=qs_t):
        return _dot(k, qs[:, blk(c)])

    def softmax_pv(s, s_max, vt, c):
        m_prev = m_ref[:, blk(c)]
        m_new = jnp.maximum(m_prev, s_max)
        alpha = jnp.exp2(m_prev - m_new)
        p = jnp.exp2(s - m_new)
        l_ref[:, blk(c)] = alpha * l_ref[:, blk(c)] + jnp.sum(p, axis=0, keepdims=True)
        m_ref[:, blk(c)] = m_new
        acc_ref[:, blk(c)] = alpha * acc_ref[:, blk(c)] + _dot(vt, p.astype(BF16))

    def ctx_chunk(c):
        s = scores(kc_ref[...], c)
        softmax_pv(s, jnp.max(s, axis=0, keepdims=True), vct_ref[...], c)

    if not use_x:
        for c in range(n_blk):
            ctx_chunk(c)
    else:
        n_chunk, _, tk = vxt_ref.shape
        buf_a, buf_b = (sa_ref, xa_ref), (sb_ref, xb_ref)

        def buffers(j):
            return (buf_a, buf_b) if j % 2 == 0 else (buf_b, buf_a)

        def k_chunk(j):
            return kx_ref[pl.ds(pl.multiple_of(j * tk, tk), tk), :]

        def scores_into(buf, k, c, qs=qs_t):
            s = scores(k, c, qs)
            buf[0][:, blk(c)] = s
            buf[1][:, blk(c)] = jnp.max(s, axis=0, keepdims=True)

        buf_ctx = (sc_ref, xc_ref)

        def half_step(j, last=False):
            cur, nxt = buffers(j)
            vt = vxt_ref[j]
            qs_next = _stack_heads_t(qn_ref, n_qp) if last else qs_t
            k_next = kn_ref[...] if last else k_chunk(j + 1)
            for c in range(n_blk):
                scores_into(nxt, k_next, c, qs_next)
                if j == 0:
                    softmax_pv(sc_ref[:, blk(c)], xc_ref[:, blk(c)], vct_ref[...], c)
                softmax_pv(cur[0][:, blk(c)], cur[1][:, blk(c)], vt, c)
                if last:
                    scores_into(buf_ctx, kcn_ref[...], c, qs_next)

        first_step = (pl.program_id(0) + pl.program_id(1) + pl.program_id(2)) == 0

        @pl.when(first_step)
        def _():
            k0 = k_chunk(0)
            for c in range(n_blk):
                scores_into(buf_a, k0, c)
                scores_into(buf_ctx, kc_ref[...], c)

        unroll = 2 * ATTN_PAIR_UNROLL
        n_loop = max(n_chunk - 2, 0) // unroll * unroll
        half_step(0, last=n_chunk == 1)

        def body(jj, carry):
            for u in range(unroll):
                cur, nxt = buffers(1 + u)
                j = 1 + unroll * jj + u
                vt = vxt_ref[j]
                k_next = k_chunk(j + 1)
                for c in range(n_blk):
                    scores_into(nxt, k_next, c)
                    softmax_pv(cur[0][:, blk(c)], cur[1][:, blk(c)], vt, c)
            return carry
        lax.fori_loop(0, n_loop // unroll, body, 0)
        for j in range(1 + n_loop, n_chunk - 1):
            half_step(j)
        if n_chunk > 1:
            half_step(n_chunk - 1, last=True)
        if n_chunk % 2:
            sa_ref[...] = sb_ref[...]
            xa_ref[...] = xb_ref[...]

    o_t = acc_ref[...] / l_ref[...]
    if diff:
        dl = dl_ref[...]
        lam = (jnp.exp(jnp.sum(dl[0:1] * dl[1:2], axis=-1, keepdims=True))
               - jnp.exp(jnp.sum(dl[2:3] * dl[3:4], axis=-1, keepdims=True)) + lam_init)
        d = (o_t[:, :tq] - lam * o_t[:, tq:]).T
        o_ref[0] = (_rms(d) * sub_ref[...] * (1.0 - lam_init)).astype(o_ref.dtype)
    else:
        for p in range(n_qp):
            o_ref[p] = _merge_heads_t(o_t, p, tq).astype(o_ref.dtype)


def _full_attn(q_cols, q_col0, rows_x, cols_x, rows_c, cols_c, k_pair0, vx_col0, vc_col0, *, n_qp, tq,
               n_groups, sink=None, diff_lambda=None, diff_subln=None, lam_init=0.0):
    bsz, _, n_qchunk, _, q_chunk = q_cols.shape
    t_q = n_qchunk * q_chunk
    per_chunk = q_chunk // tq
    c_len = rows_c.shape[2]
    use_x = rows_x is not None
    diff = diff_lambda is not None
    m_cols = 2 * n_qp * tq
    qp_blk = q_col0 // n_qp
    n_tiles = t_q // tq

    def q_index(b, g, i):
        return b, qp_blk + g, i // per_chunk, 0, i % per_chunk

    def next_step(b, g, i):
        wrap_i = (i + 1 == n_tiles).astype(jnp.int32)
        wrap_g = wrap_i * (g + 1 == n_groups).astype(jnp.int32)
        last = wrap_g * (b + 1 == bsz).astype(jnp.int32)
        return (b + wrap_g - last, (g + wrap_i) * (1 - wrap_g) + g * last, (i + 1) * (1 - wrap_i) + i * last)

    in_specs = [pl.BlockSpec((None, n_qp, None, LANES, tq), q_index)]
    args = [q_cols]
    if use_x:
        s_len = rows_x.shape[2]
        n_chunk, tk = cols_x.shape[2], cols_x.shape[4]
        in_specs += [pl.BlockSpec((None, None, s_len, LANES), lambda b, g, i: (b, k_pair0 + g, 0, 0)),
                     pl.BlockSpec((None, None, n_chunk, LANES, tk), lambda b, g, i: (b, vx_col0 + g, 0, 0, 0)),
                     pl.BlockSpec((None, n_qp, None, LANES, tq), lambda b, g, i: q_index(*next_step(b, g, i))),
                     pl.BlockSpec((None, None, tk, LANES),
                                  lambda b, g, i: (next_step(b, g, i)[0], k_pair0 + next_step(b, g, i)[1], 0, 0)),
                     pl.BlockSpec((None, None, c_len, LANES),
                                  lambda b, g, i: (next_step(b, g, i)[0], k_pair0 + next_step(b, g, i)[1], 0, 0))]
        args += [rows_x, cols_x, q_cols, rows_x, rows_c]
    in_specs += [pl.BlockSpec((None, None, c_len, LANES), lambda b, g, i: (b, k_pair0 + g, 0, 0)),
                 pl.BlockSpec((None, None, None, LANES, c_len), lambda b, g, i: (b, vc_col0 + g, 0, 0, 0))]
    args += [rows_c, cols_c]
    if sink is not None:
        in_specs.append(pl.BlockSpec(memory_space=pltpu.SMEM))
        args.append(sink)
    if diff:
        in_specs += [_const_spec((4, HEAD_DIM)), _const_spec((1, LANES))]
        args += [diff_lambda, diff_subln.reshape(1, LANES)]
    return pl.pallas_call(
        functools.partial(_full_attn_kernel, n_qp=n_qp, use_x=use_x, use_sink=sink is not None, diff=diff,
                          col_blk=min(ATTN_COL_BLOCK, m_cols), lam_init=lam_init),
        grid=(bsz, n_groups, t_q // tq),
        in_specs=in_specs,
        out_specs=pl.BlockSpec((None, n_qp, tq, LANES), lambda b, g, i: (b, g, i, 0)),
        out_shape=jax.ShapeDtypeStruct((bsz, n_groups * n_qp, t_q, LANES), BF16),
        scratch_shapes=[pltpu.VMEM((1, m_cols), F32), pltpu.VMEM((1, m_cols), F32),
                        pltpu.VMEM((LANES, m_cols), F32)]
        + ([pltpu.VMEM((cols_x.shape[4], m_cols), F32)] * 2 + [pltpu.VMEM((1, m_cols), F32)] * 2
           + [pltpu.VMEM((c_len, m_cols), F32), pltpu.VMEM((1, m_cols), F32)] if use_x else []),
        compiler_params=_params(*(("arbitrary",) * 3 if use_x else ("parallel",) * 3)),
    )(*args)


def _na_kernel(qt_ref, kx_ref, vt_ref, kc_ref, vct_ref, bias_ref, o_ref, *, grid_rows):
    j = pl.program_id(2)
    tq = qt_ref.shape[1]
    n_keys = NA_SLAB * GRID_W
    row0 = jnp.clip(NA_QROWS * j - NA_KH // 2, 0, grid_rows - NA_SLAB)
    start = pl.multiple_of(row0 * GRID_W, (NA_KH // 2) * GRID_W)
    k_slab = kx_ref[pl.ds(start, n_keys), :]
    vt_slab = _value_slab_t(vt_ref, start, n_keys)
    qt = qt_ref[...]
    top = _top_half(qt.shape)
    qs_t = jnp.concatenate([jnp.where(top, qt, jnp.zeros_like(qt)), jnp.where(top, jnp.zeros_like(qt), qt)], axis=1)
    s_loc = _dot(k_slab, qs_t) + jnp.concatenate([bias_ref[0], bias_ref[1]], axis=1)
    s_ctx = _dot(kc_ref[...], qs_t)
    m = jnp.maximum(jnp.max(s_loc, axis=0, keepdims=True), jnp.max(s_ctx, axis=0, keepdims=True))
    p_loc = jnp.exp2(s_loc - m)
    p_ctx = jnp.exp2(s_ctx - m)
    l = jnp.sum(p_loc, axis=0, keepdims=True) + jnp.sum(p_ctx, axis=0, keepdims=True)
    o_t = (_dot(vt_slab, p_loc.astype(BF16)) + _dot(vct_ref[...], p_ctx.astype(BF16))) / l
    o_ref[...] = _merge_heads_t(o_t, 0, tq).astype(o_ref.dtype)


def _na_bias_tables(na_bias, grid_rows):
    n_heads, n_dr, n_dc = na_bias.shape
    n_blk = grid_rows // NA_QROWS
    plan = []
    for j in (0, 1, n_blk - 1):
        row0 = min(max(NA_QROWS * j - NA_KH // 2, 0), grid_rows - NA_SLAB)
        rows = []
        for i in range(NA_QROWS):
            r = NA_QROWS * j + i
            r0 = min(max(r - NA_KH // 2, 0), grid_rows - NA_KH)
            rows.append(tuple((row0 + t - r + NA_KH - 1) if r0 <= row0 + t < r0 + NA_KH else None
                              for t in range(NA_SLAB)))
        plan.append(tuple(rows))
    padded = jnp.pad(jnp.flip(na_bias, axis=2), ((0, 0), (0, 16 - n_dr), (0, LANES - n_dc)))
    return pl.pallas_call(
        functools.partial(_na_bias_kernel, plan=tuple(plan)),
        grid=(n_heads,),
        in_specs=[pl.BlockSpec((None, 16, LANES), lambda h: (h, 0, 0))],
        out_specs=pl.BlockSpec((3, None, NA_SLAB * GRID_W, NA_QROWS * GRID_W), lambda h: (0, h, 0, 0)),
        out_shape=jax.ShapeDtypeStruct((3, n_heads, NA_SLAB * GRID_W, NA_QROWS * GRID_W), F32),
        compiler_params=_params("parallel"),
    )(padded)


def _na_bias_kernel(b_ref, o_ref, *, plan):
    shape = (GRID_W, LANES)
    lane = lax.broadcasted_iota(jnp.int32, shape, 1)
    kc = lax.broadcasted_iota(jnp.int32, shape, 0)
    cq = lane % GRID_W
    c0 = jnp.clip(cq - NA_KW // 2, 0, GRID_W - NA_KW)
    col_bias = jnp.where(kc >= c0, jnp.where(kc < c0 + NA_KW, 0.0, NEG), NEG).astype(F32)
    left_half = lane < GRID_W
    neg = jnp.full(shape, NEG, F32)
    used = sorted({dr for cfg in plan for row in cfg for dr in row if dr is not None})
    shift = LANES - (NA_KW - 1)
    tile_lo, tile_hi = {}, {}
    for dr in used:
        row = jnp.broadcast_to(b_ref[dr:dr + 1, :] * LOG2_E, shape)
        tile_lo[dr] = pltpu.roll(row, shift, 1, stride=1, stride_axis=0)
        tile_hi[dr] = pltpu.roll(row, (shift + GRID_W) % LANES, 1, stride=1, stride_axis=0)
    for cfg, cfg_rows in enumerate(plan):
        for t in range(NA_SLAB):
            for u in range(NA_QROWS // 2):
                dr_left, dr_right = cfg_rows[2 * u][t], cfg_rows[2 * u + 1][t]
                left = neg if dr_left is None else tile_lo[dr_left]
                right = neg if dr_right is None else tile_hi[dr_right]
                blk = jnp.where(col_bias < 0.0, neg, jnp.where(left_half, left, right))
                o_ref[cfg, t * GRID_W:(t + 1) * GRID_W, u * LANES:(u + 1) * LANES] = blk


def _na_attn(qkv_x, colx, v128_x, qkv_c, colc, bias_tabs):
    bsz, _, s_len, _ = qkv_x.shape
    c_len = qkv_c.shape[2]
    grid_rows = s_len // GRID_W
    n_blk = grid_rows // NA_QROWS
    tq = NA_QROWS * GRID_W
    n_keys = NA_SLAB * GRID_W
    per_chunk = colx.shape[4] // tq
    q0, v0, vc0 = COLS_X.index(P_BQ), COLS128_X.index(P_BV), COLS_C.index(P_BV)

    def cfg(j):
        return jnp.where(j == 0, 0, jnp.where(j == n_blk - 1, 2, 1))

    return pl.pallas_call(
        functools.partial(_na_kernel, grid_rows=grid_rows),
        grid=(bsz, 4, n_blk),
        in_specs=[pl.BlockSpec((None, None, None, LANES, tq),
                               lambda b, p, j: (b, q0 + p, j // per_chunk, 0, j % per_chunk)),
                  pl.BlockSpec((None, None, s_len, LANES), lambda b, p, j: (b, P_BK + p, 0, 0)),
                  pl.BlockSpec((None, None, s_len // LANES, LANES, LANES), lambda b, p, j: (b, v0 + p, 0, 0, 0)),
                  pl.BlockSpec((None, None, c_len, LANES), lambda b, p, j: (b, P_BK + p, 0, 0)),
                  pl.BlockSpec((None, None, None, LANES, c_len), lambda b, p, j: (b, vc0 + p, 0, 0, 0)),
                  pl.BlockSpec((None, 2, n_keys, tq), lambda b, p, j: (cfg(j), p, 0, 0))],
        out_specs=pl.BlockSpec((None, None, tq, LANES), lambda b, p, j: (b, p, j, 0)),
        out_shape=jax.ShapeDtypeStruct((bsz, 4, s_len, LANES), BF16),
        compiler_params=_params("parallel", "parallel", "parallel"),
    )(colx, qkv_x, v128_x, qkv_c, colc, bias_tabs)


def _swa_kernel(qt_ref, kx_ref, vt_ref, kc_ref, vct_ref, sink_ref, o_ref):
    n_qp, _, tq = qt_ref.shape
    n_heads = 2 * n_qp
    s_len = kx_ref.shape[0]
    n_keys = tq + 2 * SWA_WINDOW
    q0 = pl.program_id(2) * tq
    start = pl.multiple_of(jnp.clip(q0 - SWA_WINDOW, 0, s_len - n_keys), SWA_WINDOW)
    k_slab = kx_ref[pl.ds(start, n_keys), :]
    vt_slab = _value_slab_t(vt_ref, start, n_keys)
    qs_t = _stack_heads_t(qt_ref, n_qp)
    kpos = start + lax.broadcasted_iota(jnp.int32, (n_keys, tq), 0)
    qpos = q0 + lax.broadcasted_iota(jnp.int32, (n_keys, tq), 1)
    band = jnp.where(jnp.abs(kpos - qpos) <= SWA_WINDOW, 0.0, NEG).astype(F32)
    s_loc = _dot(k_slab, qs_t) + jnp.concatenate([band] * n_heads, axis=1)
    s_ctx = _dot(kc_ref[...], qs_t)
    sink = _sink_row(sink_ref, pl.program_id(1), n_heads, tq)
    m = jnp.maximum(jnp.maximum(jnp.max(s_loc, axis=0, keepdims=True), jnp.max(s_ctx, axis=0, keepdims=True)),
                    sink)
    p_loc = jnp.exp2(s_loc - m)
    p_ctx = jnp.exp2(s_ctx - m)
    l = jnp.sum(p_loc, axis=0, keepdims=True) + jnp.sum(p_ctx, axis=0, keepdims=True) + jnp.exp2(sink - m)
    o_t = (_dot(vt_slab, p_loc.astype(BF16)) + _dot(vct_ref[...], p_ctx.astype(BF16))) / l
    for p in range(n_qp):
        o_ref[p] = _merge_heads_t(o_t, p, tq).astype(o_ref.dtype)


def _swa_attn(qkv_x, colx, v128_x, qkv_c, colc, sink):
    bsz, _, s_len, _ = qkv_x.shape
    c_len = qkv_c.shape[2]
    n_qp = 2
    tq = min(SWA_TQ, s_len)
    per_chunk = colx.shape[4] // tq
    q0, v0, vc0 = COLS_X.index(P_DQ) // n_qp, COLS128_X.index(P_DV), COLS_C.index(P_DV)
    return pl.pallas_call(
        _swa_kernel,
        grid=(bsz, 2, s_len // tq),
        in_specs=[pl.BlockSpec((None, n_qp, None, LANES, tq),
                               lambda b, g, i: (b, q0 + g, i // per_chunk, 0, i % per_chunk)),
                  pl.BlockSpec((None, None, s_len, LANES), lambda b, g, i: (b, P_DK + g, 0, 0)),
                  pl.BlockSpec((None, None, s_len // LANES, LANES, LANES), lambda b, g, i: (b, v0 + g, 0, 0, 0)),
                  pl.BlockSpec((None, None, c_len, LANES), lambda b, g, i: (b, P_DK + g, 0, 0)),
                  pl.BlockSpec((None, None, None, LANES, c_len), lambda b, g, i: (b, vc0 + g, 0, 0, 0)),
                  pl.BlockSpec(memory_space=pltpu.SMEM)],
        out_specs=pl.BlockSpec((None, n_qp, tq, LANES), lambda b, g, i: (b, g, i, 0)),
        out_shape=jax.ShapeDtypeStruct((bsz, 2 * n_qp, s_len, LANES), BF16),
        compiler_params=_params("parallel", "parallel", "parallel"),
    )(colx, qkv_x, v128_x, qkv_c, colc, sink)


def _merge_kernel(h_ref, mod_ref, g_ref, ya_ref, yb_ref, yc_ref, yd_ref, wgate_ref, wbr_ref, wout_ref, o_ref):
    h = h_ref[...]
    mod = mod_ref[...]
    g = g_ref[...]
    d = h.shape[1]
    u = (_rms(h) * g[2:3] * (1.0 + mod[4:5]) + mod[3:4]).astype(BF16)
    acc = None
    for i, y_ref in enumerate((ya_ref, yb_ref, yc_ref, yd_ref)):
        y = jnp.concatenate([y_ref[p] for p in range(y_ref.shape[0])], axis=-1)
        term = _sigmoid(_dot(u, wgate_ref[:, i * d:(i + 1) * d])) * _dot(y, wbr_ref[i])
        acc = term if acc is None else acc + term
    out = _dot(acc.astype(BF16), wout_ref[...])
    o_ref[...] = h + mod[5:6] * (_rms(out) * g[3:4])


def _merge(h, mod, gains, ys, w_gate, w_branch, w_out, *, mod_row, tm):
    bsz, t, d = h.shape
    n_yp = ys[0].shape[1]
    y_spec = pl.BlockSpec((None, n_yp, tm, LANES), lambda b, i: (b, 0, i, 0))
    return pl.pallas_call(
        _merge_kernel,
        grid=(bsz, t // tm),
        in_specs=[pl.BlockSpec((None, tm, d), lambda b, i: (b, i, 0)),
                  pl.BlockSpec((None, N_MOD, d), lambda b, i: (mod_row(b), 0, 0)),
                  _const_spec((6, d)), y_spec, y_spec, y_spec, y_spec,
                  _const_spec(w_gate.shape), _const_spec(w_branch.shape), _const_spec(w_out.shape)],
        out_specs=pl.BlockSpec((None, tm, d), lambda b, i: (b, i, 0)),
        out_shape=jax.ShapeDtypeStruct((bsz, t, d), F32),
        compiler_params=_params("parallel", "parallel"),
    )(h, mod, gains, *ys, w_gate, w_branch, w_out)


def _rope_tables(n_tok):
    t = jnp.arange(n_tok, dtype=jnp.int32)
    row = (t // GRID_W).astype(F32)
    col = (t % GRID_W).astype(F32)
    n_freq = HEAD_DIM // 4
    inv_freq = ROPE_THETA ** (-jnp.arange(n_freq, dtype=F32) / n_freq)
    lane = jnp.arange(LANES, dtype=jnp.int32)
    axis = (lane % HEAD_DIM) // (HEAD_DIM // 2)
    freq = inv_freq[lane % n_freq]
    ang = jnp.where(axis[None, :] == 0, row[:, None], col[:, None]) * freq[None, :]
    first = (lane % (HEAD_DIM // 2)) < n_freq
    cos, sin = jnp.cos(ang), jnp.sin(ang)
    return cos, jnp.where(first[None, :], -sin, 0.0), jnp.where(first[None, :], 0.0, sin)


def _dup_heads(w):
    h0, h1 = w[:, :HEAD_DIM], w[:, HEAD_DIM:]
    return jnp.concatenate([h0, h0, h1, h1], axis=1)


def _qkv_weight(w_in):
    o = 0
    cols = []
    for width in (1536, 1536):
        cols.append(w_in[:, o:o + width])
        o += width
    for _ in range(2):
        cols += [w_in[:, o:o + 512], _dup_heads(w_in[:, o + 512:o + 640]), _dup_heads(w_in[:, o + 640:o + 768])]
        o += 768
    return jnp.concatenate(cols, axis=1).astype(BF16), o


def kernel(x, c, ctx, c_ctx, w_mod, b_mod, norm_gain, ffn_w_gate, ffn_w_up, ffn_w_down, w_in, w_branch,
           w_out, diff_lambda, diff_subln, na_bias, qk_norm, sink):
    bsz, s_len, d = x.shape
    c_len = ctx.shape[1]
    depth = w_mod.shape[0]
    tm = min(512, s_len)
    cond_t = jnp.zeros((d, 8), F32).at[:, :bsz].set(c.T).at[:, bsz].set(c_ctx)
    rope_tabs = _rope_tables(s_len)
    grp = jnp.arange(LANES, dtype=jnp.int32) // HEAD_DIM
    gmat = jnp.where(grp[:, None] == grp[None, :], 1.0 / HEAD_DIM, 0.0).astype(BF16)
    x_row = lambda b: b
    c_row = lambda b: bsz
    cx, cc = COLS_X.index, COLS_C.index

    h_x, h_c = x, ctx
    for l in range(depth):
        last = l == depth - 1
        lam_init = 0.8 - 0.6 * math.exp(-0.3 * l)
        mod = _modulation(cond_t, w_mod[l], b_mod[l], bsz + 1).reshape(bsz + 1, N_MOD, d)
        g = norm_gain[l]
        wg, wu, wd = ffn_w_gate[l].astype(BF16), ffn_w_up[l].astype(BF16), ffn_w_down[l].astype(BF16)
        w_qkv, gate_off = _qkv_weight(w_in[l])
        w_gate = w_in[l][:, gate_off:].astype(BF16)
        w_br = w_branch[l].astype(BF16)
        w_o = w_out[l].astype(BF16)
        qk_gain = jnp.tile(qk_norm[l], (1, 2))

        h_x = _ffn(h_x, mod, g, wg[0], wu[0], wd[0], k=0, mod_row=x_row, tm=tm)
        h_c = _ffn(h_c, mod, g, wg[0], wu[0], wd[0], k=0, mod_row=c_row, tm=c_len)

        qkv_x, colx, v128_x = _inproj(h_x, mod, g, w_qkv, qk_gain, gmat, rope_tabs, mod_row=x_row, tm=tm,
                                      cols=COLS_X, cols128=COLS128_X)
        qkv_c, colc = _inproj(h_c, mod, g, w_qkv, qk_gain, gmat, None, mod_row=c_row, tm=c_len, cols=COLS_C)

        ya = _full_attn(colx, cx(P_AQ), qkv_x, colx, qkv_c, colc, P_AK, cx(P_AV), cc(P_AV), n_qp=1,
                        tq=min(512, s_len), n_groups=4,
                        diff_lambda=diff_lambda[l], diff_subln=diff_subln[l], lam_init=lam_init)
        yb = _na_attn(qkv_x, colx, v128_x, qkv_c, colc, _na_bias_tables(na_bias[l], s_len // GRID_W))
        yc = _full_attn(colx, cx(P_CQ), qkv_x, colx, qkv_c, colc, P_CK, cx(P_CV), cc(P_CV), n_qp=2,
                        tq=min(256, s_len), n_groups=2)
        yd = _swa_attn(qkv_x, colx, v128_x, qkv_c, colc, sink[l])
        h_x = _merge(h_x, mod, g, (ya, yb, yc, yd), w_gate, w_br, w_o, mod_row=x_row, tm=tm)
        h_x = _ffn(h_x, mod, g, wg[1], wu[1], wd[1], k=2, mod_row=x_row, tm=tm)

        if not last:
            ya = _full_attn(colc, cc(P_AQ), None, None, qkv_c, colc, P_AK, None, cc(P_AV), n_qp=1, tq=c_len,
                            n_groups=4, diff_lambda=diff_lambda[l], diff_subln=diff_subln[l], lam_init=lam_init)
            yb = _full_attn(colc, cc(P_BQ), None, None, qkv_c, colc, P_BK, None, cc(P_BV), n_qp=1, tq=c_len,
                            n_groups=4)
            yc = _full_attn(colc, cc(P_CQ), None, None, qkv_c, colc, P_CK, None, cc(P_CV), n_qp=2, tq=c_len,
                            n_groups=2)
            yd = _full_attn(colc, cc(P_DQ), None, None, qkv_c, colc, P_DK, None, cc(P_DV), n_qp=2, tq=c_len,
                            n_groups=2, sink=sink[l])
            h_c = _merge(h_c, mod, g, (ya, yb, yc, yd), w_gate, w_br, w_o, mod_row=c_row, tm=c_len)
            h_c = _ffn(h_c, mod, g, wg[1], wu[1], wd[1], k=2, mod_row=c_row, tm=c_len)
    return h_x
```
